```python
import math
import jax
import jax.numpy as jnp
from jax import lax
import numpy as np

D_MODEL = 2048
BATCH = 8
SEQ = 2048
DEPTH = 4
DEC_BATCH = 16
DEC_SEQ = 2048
PAST_LEN = 128

EPS = 1e-6
HEAD_DIM = 128
HEADS_PER_GROUP = 4
DILATED_GROUPS = ((128, 1), (512, 4), (2048, 16))
N_ATTN_HEADS = HEADS_PER_GROUP * len(DILATED_GROUPS)
ATTN_WIDTH = N_ATTN_HEADS * HEAD_DIM
ATTN_OUT_WIDTH = HEADS_PER_GROUP * HEAD_DIM
ATTN_BLOCK = 64
ALIBI_SLOPES = tuple(2.0 ** (-8.0 * (h + 1) / N_ATTN_HEADS) for h in range(N_ATTN_HEADS))
POOL_WINDOWS = (2, 4, 8, 16)
POOL_WIDTH = D_MODEL // 2
POOL_GROUP = POOL_WIDTH // len(POOL_WINDOWS)
SSM_WIDTH = D_MODEL // 2
SSM_GROUP = 16
SSM_GROUPS = SSM_WIDTH // SSM_GROUP
SSM_STATE = 64
SSM_DT_MIN = 1e-3
SSM_DT_MAX = 1e-1
N_BRANCHES = 3
IN_WIDTH = 3 * ATTN_WIDTH + POOL_WIDTH + SSM_WIDTH + N_BRANCHES * D_MODEL
N_EXPERTS = 16
EXPERT_FF = 11 * D_MODEL // 8
CAPACITY_FACTOR = 2

kernel_name = "hybrid_dilated_pool_s5_ec_encoder"


def rms_norm(x, g):
    xf = x.astype(jnp.float32)
    y = xf * lax.rsqrt(jnp.mean(xf * xf, axis=-1, keepdims=True) + EPS)
    return (y * g.astype(jnp.float32)).astype(x.dtype)


def dilated_group_attention(q, k, v, slopes, window, dil):
    B, L, H, Dh = q.shape
    n = L // dil
    half = window // (2 * dil)
    nb = -(-n // ATTN_BLOCK)
    pad = nb * ATTN_BLOCK - n

    def to_sub(t):
        return t.reshape(B, n, dil, H, Dh).transpose(0, 2, 3, 1, 4)

    qs, ks, vs = to_sub(q), to_sub(k), to_sub(v)
    qb = jnp.pad(qs, ((0, 0), (0, 0), (0, 0), (0, pad), (0, 0))).reshape(B, dil, H, nb, ATTN_BLOCK, Dh)

    def key_blocks(t):
        tp = jnp.pad(t, ((0, 0), (0, 0), (0, 0), (ATTN_BLOCK, ATTN_BLOCK + pad), (0, 0)))
        tp = tp.reshape(B, dil, H, nb + 2, ATTN_BLOCK, Dh)
        return jnp.concatenate([tp[:, :, :, :-2], tp[:, :, :, 1:-1], tp[:, :, :, 2:]], axis=4)

    kb, vb = key_blocks(ks), key_blocks(vs)
    scores = jnp.einsum('brhnqd,brhnkd->brhnqk', qb, kb).astype(jnp.float32) * (Dh ** -0.5)
    blk = jnp.arange(nb)[:, None, None] * ATTN_BLOCK
    qpos = blk + jnp.arange(ATTN_BLOCK)[None, :, None]
    kpos = blk - ATTN_BLOCK + jnp.arange(3 * ATTN_BLOCK)[None, None, :]
    rel = jnp.abs(kpos - qpos)
    valid = (rel <= half) & (kpos >= 0) & (kpos < n)
    bias = -(slopes * dil)[:, None, None, None] * rel.astype(jnp.float32)
    scores = jnp.where(valid, scores + bias, -jnp.inf)
    lse = jax.nn.logsumexp(scores, axis=-1)
    p = jnp.exp(scores - lse[..., None])
    o = jnp.einsum('brhnqk,brhnkd->brhnqd', p.astype(v.dtype), vb)
    o = o.reshape(B, dil, H, nb * ATTN_BLOCK, Dh)[:, :, :, :n]
    lse = lse.reshape(B, dil, H, nb * ATTN_BLOCK)[:, :, :, :n]
    o = o.transpose(0, 3, 1, 2, 4).reshape(B, L, H, Dh)
    lse = lse.transpose(0, 3, 1, 2).reshape(B, L, H)
    return o, lse


def dilated_attention(q, k, v):
    B, L, _ = q.shape
    q = q.reshape(B, L, N_ATTN_HEADS, HEAD_DIM)
    k = k.reshape(B, L, N_ATTN_HEADS, HEAD_DIM)
    v = v.reshape(B, L, N_ATTN_HEADS, HEAD_DIM)
    outs, lses = [], []
    for gi, (window, dil) in enumerate(DILATED_GROUPS):
        hs = slice(gi * HEADS_PER_GROUP, (gi + 1) * HEADS_PER_GROUP)
        slopes = jnp.asarray(ALIBI_SLOPES[hs], jnp.float32)
        o, l = dilated_group_attention(q[:, :, hs], k[:, :, hs], v[:, :, hs], slopes, window, dil)
        outs.append(o)
        lses.append(l)
    o = jnp.stack(outs, axis=2)
    w = jax.nn.softmax(jnp.stack(lses, axis=2), axis=2)
    y = jnp.sum(o * w[..., None].astype(o.dtype), axis=2)
    return y.reshape(B, L, ATTN_OUT_WIDTH)


def multiscale_pool(u, pool_w, pool_scale):
    B, L, C = u.shape
    uf = u.astype(jnp.float32)
    cs = jnp.concatenate([jnp.zeros((B, 1, C), jnp.float32), jnp.cumsum(uf, axis=1)], axis=1)
    t = jnp.arange(L)
    outs = []
    for gi, w in enumerate(POOL_WINDOWS):
        lo = jnp.clip(t - w // 2, 0, L - 1)
        hi = jnp.clip(t + w // 2 - 1, 0, L - 1)
        cg = cs[:, :, gi * POOL_GROUP:(gi + 1) * POOL_GROUP]
        cnt = (hi - lo + 1).astype(jnp.float32)[None, :, None]
        pooled = (cg[:, hi + 1] - cg[:, lo]) / cnt - uf[:, :, gi * POOL_GROUP:(gi + 1) * POOL_GROUP]
        outs.append(jnp.einsum('blc,cd->bld', pooled.astype(u.dtype), pool_w[gi]))
    return jnp.concatenate(outs, axis=-1) * pool_scale


def _complex_linear_combine(e1, e2):
    a1r, a1i, b1r, b1i = e1
    a2r, a2i, b2r, b2i = e2
    return (a2r * a1r - a2i * a1i,
            a2r * a1i + a2i * a1r,
            a2r * b1r - a2i * b1i + b2r,
            a2r * b1i + a2i * b1r + b2i)


def bidirectional_s5(u, a_re, a_im, log_step, b_re, b_im, c_re, c_im, d_skip, glu_w):
    B, L, W = u.shape
    uf = u.astype(jnp.float32)
    ug = uf.reshape(B, L, SSM_GROUPS, SSM_GROUP)
    ys = jnp.zeros((B, L, SSM_GROUPS, SSM_GROUP), jnp.float32)
    for direction in range(2):
        lr = a_re[direction].astype(jnp.float32)
        li = a_im[direction].astype(jnp.float32)
        dt = jnp.exp(log_step[direction].astype(jnp.float32))[:, None]
        mag = jnp.exp(lr * dt)
        ab_r, ab_i = mag * jnp.cos(li * dt), mag * jnp.sin(li * dt)
        den = lr * lr + li * li
        xr, xi = ab_r - 1.0, ab_i
        co_r = (xr * lr + xi * li) / den
        co_i = (xi * lr - xr * li) / den
        br = b_re[direction].astype(jnp.float32)
        bi = b_im[direction].astype(jnp.float32)
        bb_r = co_r[..., None] * br - co_i[..., None] * bi
        bb_i = co_r[..., None] * bi + co_i[..., None] * br
        bu_r = jnp.einsum('blgh,gph->blgp', ug, bb_r)
        bu_i = jnp.einsum('blgh,gph->blgp', ug, bb_i)
        a_r = jnp.broadcast_to(ab_r, (1, L) + ab_r.shape)
        a_i = jnp.broadcast_to(ab_i, (1, L) + ab_i.shape)
        _, _, h_r, h_i = lax.associative_scan(
            _complex_linear_combine, (a_r, a_i, bu_r, bu_i), reverse=(direction == 1), axis=1)
        cr = c_re[direction].astype(jnp.float32)
        ci = c_im[direction].astype(jnp.float32)
        ys = ys + jnp.einsum('blgp,ghp->blgh', h_r, cr) - jnp.einsum('blgp,ghp->blgh', h_i, ci)
    y = ys.reshape(B, L, W) + d_skip.astype(jnp.float32) * uf
    g = jax.nn.gelu(y)
    out = g * jax.nn.sigmoid(g @ glu_w.astype(jnp.float32))
    return out.astype(u.dtype)


def mixer_block(x, norm_g, w_in, pool_w, pool_scale, a_re, a_im, log_step, b_re, b_im,
                c_re, c_im, d_skip, glu_w, w_br_attn, w_br_pool, w_br_ssm, w_out):
    h = rms_norm(x, norm_g)
    proj = h @ w_in
    cuts = np.cumsum([ATTN_WIDTH, ATTN_WIDTH, ATTN_WIDTH, POOL_WIDTH, SSM_WIDTH, D_MODEL, D_MODEL]).tolist()
    q, k, v, u_pool, u_ssm, g_attn, g_pool, g_ssm = jnp.split(proj, cuts, axis=-1)
    y_attn = dilated_attention(q, k, v) @ w_br_attn
    y_pool = multiscale_pool(u_pool, pool_w, pool_scale) @ w_br_pool
    y_ssm = bidirectional_s5(u_ssm, a_re, a_im, log_step, b_re, b_im, c_re, c_im, d_skip, glu_w) @ w_br_ssm
    merged = (jax.nn.sigmoid(g_attn) * y_attn + jax.nn.sigmoid(g_pool) * y_pool
              + jax.nn.sigmoid(g_ssm) * y_ssm)
    return x + merged @ w_out


def expert_choice_block(x, norm_g, router_w, w_gate, w_up, w_down):
    B, L, D = x.shape
    T = B * L
    cap = CAPACITY_FACTOR * T // N_EXPERTS
    h = rms_norm(x, norm_g).reshape(T, D)
    aff = jax.nn.softmax((h @ router_w).astype(jnp.float32), axis=-1)
    gate, idx = lax.top_k(aff.T, cap)
    xin = h[idx]
    hid = jax.nn.silu(jnp.einsum('ecd,edf->ecf', xin, w_gate)) * jnp.einsum('ecd,edf->ecf', xin, w_up)
    out = jnp.einsum('ecf,efd->ecd', hid, w_down) * gate[..., None].astype(x.dtype)
    y = jnp.zeros((T, D), x.dtype).at[idx.reshape(-1)].add(out.reshape(-1, D))
    return x + y.reshape(B, L, D)


def encoder(x, norm1_g, w_in, pool_w, pool_scale, ssm_a_re, ssm_a_im, ssm_log_step, ssm_b_re,
            ssm_b_im, ssm_c_re, ssm_c_im, ssm_d, ssm_glu_w, w_br_attn, w_br_pool, w_br_ssm, w_out,
            norm2_g, router_w, w_gate, w_up, w_down, final_norm_g):
    for l in range(DEPTH):
        x = mixer_block(x, norm1_g[l], w_in[l], pool_w[l], pool_scale[l], ssm_a_re[l], ssm_a_im[l],
                        ssm_log_step[l], ssm_b_re[l], ssm_b_im[l], ssm_c_re[l], ssm_c_im[l], ssm_d[l],
                        ssm_glu_w[l], w_br_attn[l], w_br_pool[l], w_br_ssm[l], w_out[l])
        x = expert_choice_block(x, norm2_g[l], router_w[l], w_gate[l], w_up[l], w_down[l])
    return rms_norm(x, final_norm_g)


def setup_inputs(seed: int = 0) -> dict:
    key = jax.random.key(seed)
    ks = jax.random.split(key, 32)
    f32 = jnp.float32

    def nrm(k, shape, scale):
        return jax.random.normal(k, shape, f32) * scale

    G, P, H = SSM_GROUPS, SSM_STATE, SSM_GROUP
    n_idx = jnp.arange(P, dtype=f32)
    return {
        "x_prompt": nrm(ks[0], (BATCH, SEQ, D_MODEL), 1.0),
        "x_sample": nrm(ks[1], (DEC_BATCH, DEC_SEQ, D_MODEL), 1.0),
        "norm1_g": 1.0 + nrm(ks[2], (DEPTH, D_MODEL), 0.02),
        "w_in": nrm(ks[3], (DEPTH, D_MODEL, IN_WIDTH), D_MODEL ** -0.5),
        "pool_w": nrm(ks[4], (DEPTH, len(POOL_WINDOWS), POOL_GROUP, POOL_GROUP), POOL_GROUP ** -0.5),
        "pool_scale": 1.0 + nrm(ks[5], (DEPTH, POOL_WIDTH), 0.05),
        "ssm_a_re": -0.5 + nrm(ks[6], (DEPTH, 2, G, P), 0.01),
        "ssm_a_im": jnp.pi * n_idx + nrm(ks[7], (DEPTH, 2, G, P), 0.01),
        "ssm_log_step": jax.random.uniform(ks[8], (DEPTH, 2, G), f32,
                                           math.log(SSM_DT_MIN), math.log(SSM_DT_MAX)),
        "ssm_b_re": nrm(ks[9], (DEPTH, 2, G, P, H), (2 * H) ** -0.5),
        "ssm_b_im": nrm(ks[10], (DEPTH, 2, G, P, H), (2 * H) ** -0.5),
        "ssm_c_re": nrm(ks[11], (DEPTH, 2, G, H, P), P ** -0.5),
        "ssm_c_im": nrm(ks[12], (DEPTH, 2, G, H, P), P ** -0.5),
        "ssm_d": nrm(ks[13], (DEPTH, SSM_WIDTH), 1.0),
        "ssm_glu_w": nrm(ks[14], (DEPTH, SSM_WIDTH, SSM_WIDTH), SSM_WIDTH ** -0.5),
        "w_br_attn": nrm(ks[15], (DEPTH, ATTN_OUT_WIDTH, D_MODEL), ATTN_OUT_WIDTH ** -0.5),
        "w_br_pool": nrm(ks[16], (DEPTH, POOL_WIDTH, D_MODEL), POOL_WIDTH ** -0.5),
        "w_br_ssm": nrm(ks[17], (DEPTH, SSM_WIDTH, D_MODEL), SSM_WIDTH ** -0.5),
        "w_out": nrm(ks[18], (DEPTH, D_MODEL, D_MODEL), D_MODEL ** -0.5),
        "norm2_g": 1.0 + nrm(ks[19], (DEPTH, D_MODEL), 0.02),
        "router_w": nrm(ks[20], (DEPTH, D_MODEL, N_EXPERTS), D_MODEL ** -0.5),
        "w_gate": nrm(ks[21], (DEPTH, N_EXPERTS, D_MODEL, EXPERT_FF), D_MODEL ** -0.5),
        "w_up": nrm(ks[22], (DEPTH, N_EXPERTS, D_MODEL, EXPERT_FF), D_MODEL ** -0.5),
        "w_down": nrm(ks[23], (DEPTH, N_EXPERTS, EXPERT_FF, D_MODEL), EXPERT_FF ** -0.5),
        "final_norm_g": 1.0 + nrm(ks[24], (D_MODEL,), 0.02),
    }


def reference(x_prompt, x_sample, norm1_g, w_in, pool_w, pool_scale, ssm_a_re, ssm_a_im,
              ssm_log_step, ssm_b_re, ssm_b_im, ssm_c_re, ssm_c_im, ssm_d, ssm_glu_w, w_br_attn,
              w_br_pool, w_br_ssm, w_out, norm2_g, router_w, w_gate, w_up, w_down, final_norm_g):
    weights = (norm1_g, w_in, pool_w, pool_scale, ssm_a_re, ssm_a_im, ssm_log_step, ssm_b_re,
               ssm_b_im, ssm_c_re, ssm_c_im, ssm_d, ssm_glu_w, w_br_attn, w_br_pool, w_br_ssm,
               w_out, norm2_g, router_w, w_gate, w_up, w_down, final_norm_g)
    y_prompt = encoder(x_prompt, *weights)
    y_sample = encoder(x_sample, *weights)
    return (y_prompt, y_sample)
```

```python
import functools
import math

import jax
import jax.numpy as jnp
from jax import lax
from jax.experimental import pallas as pl
from jax.experimental.pallas import tpu as pltpu

F32 = jnp.float32
BF16 = jnp.bfloat16

D_MODEL = 2048
SEQ = 2048
DEPTH = 4
EPS = 1e-6
HEAD_DIM = 128
HEADS_PER_GROUP = 4
DILATED_GROUPS = ((128, 1), (512, 4), (2048, 16))
N_ATTN_HEADS = HEADS_PER_GROUP * len(DILATED_GROUPS)
ATTN_WIDTH = N_ATTN_HEADS * HEAD_DIM
ATTN_OUT_WIDTH = HEADS_PER_GROUP * HEAD_DIM
ALIBI_SLOPES = tuple(2.0 ** (-8.0 * (h + 1) / N_ATTN_HEADS) for h in range(N_ATTN_HEADS))
POOL_WINDOWS = (2, 4, 8, 16)
POOL_WIDTH = D_MODEL // 2
POOL_GROUP = POOL_WIDTH // len(POOL_WINDOWS)
SSM_WIDTH = D_MODEL // 2
SSM_GROUP = 16
SSM_GROUPS = SSM_WIDTH // SSM_GROUP
SSM_STATE = 64
IN_WIDTH = 3 * ATTN_WIDTH + POOL_WIDTH + SSM_WIDTH + 3 * D_MODEL
N_EXPERTS = 16
EXPERT_FF = 11 * D_MODEL // 8
CAPACITY_FACTOR = 2

POOL_OFF = 0
SSM_OFF = POOL_OFF + POOL_WIDTH
Q_OFF = SSM_OFF + SSM_WIDTH
K_OFF = Q_OFF + ATTN_WIDTH
V_OFF = K_OFF + ATTN_WIDTH
GATE_OFF = V_OFF + ATTN_WIDTH

LANE = 128
SSM_CHUNK = LANE
N_CHUNKS = SEQ // SSM_CHUNK
ATTN_QB = 128
ATTN_HALF = 64
VMEM_LIMIT = 48 * 1024 * 1024


def _params(*sem):
    return pltpu.CompilerParams(dimension_semantics=sem, vmem_limit_bytes=VMEM_LIMIT)


def _inproj_kernel(x_ref, g_ref, w_ref, o_ref, h_ref):
    @pl.when(pl.program_id(1) == 0)
    def _():
        x = x_ref[...]
        ms = jnp.mean(x * x, axis=-1, keepdims=True)
        h_ref[...] = (x * lax.rsqrt(ms + EPS) * g_ref[...]).astype(BF16)

    o_ref[...] = jnp.dot(h_ref[...], w_ref[...], preferred_element_type=F32).astype(o_ref.dtype)


def _inproj(x, g, w, tm=1024, tn=512):
    m, d = x.shape
    n = w.shape[1]
    return pl.pallas_call(
        _inproj_kernel,
        grid=(m // tm, n // tn),
        in_specs=[
            pl.BlockSpec((tm, d), lambda i, j: (i, 0)),
            pl.BlockSpec((1, d), lambda i, j: (0, 0)),
            pl.BlockSpec((d, tn), lambda i, j: (0, j)),
        ],
        out_specs=pl.BlockSpec((tm, tn), lambda i, j: (i, j)),
        out_shape=jax.ShapeDtypeStruct((m, n), BF16),
        scratch_shapes=[pltpu.VMEM((tm, d), BF16)],
        compiler_params=_params("parallel", "arbitrary"),
        name="inproj",
    )(x, g.reshape(1, d), w)


def _attn_kernel(slope_ref, q_ref, k_ref, v_ref, o_ref, l_ref, *, n, dil):
    kw = min(2 * ATTN_QB, n)
    nqb = n // ATTN_QB
    neg_slope = -slope_ref[...][:, :kw] * float(dil)
    scale = HEAD_DIM ** -0.5

    def body(j, carry):
        qs = pl.multiple_of(j * ATTN_QB, ATTN_QB)
        ks = pl.multiple_of(jnp.clip(qs - ATTN_HALF, 0, n - kw), ATTN_HALF)
        q = q_ref[pl.ds(qs, ATTN_QB), :]
        k = k_ref[pl.ds(ks, kw), :]
        v = v_ref[pl.ds(ks, kw), :]
        s = lax.dot_general(q, k, (((1,), (1,)), ((), ())), preferred_element_type=F32) * scale
        qpos = qs + lax.broadcasted_iota(jnp.int32, (ATTN_QB, kw), 0)
        kpos = ks + lax.broadcasted_iota(jnp.int32, (ATTN_QB, kw), 1)
        rel = jnp.abs(kpos - qpos)
        s = jnp.where(rel <= ATTN_HALF, s + neg_slope * rel.astype(F32), -jnp.inf)
        m = jnp.max(s, axis=1, keepdims=True)
        p = jnp.exp(s - m)
        l = jnp.sum(p, axis=1, keepdims=True)
        o = jnp.dot(p.astype(BF16), v, preferred_element_type=F32) / l
        o_ref[pl.ds(qs, ATTN_QB), :] = o.astype(o_ref.dtype)
        l_ref[pl.ds(qs, ATTN_QB), :] = jnp.broadcast_to(m + jnp.log(l), (ATTN_QB, HEAD_DIM))
        return carry

    lax.fori_loop(0, nqb, body, 0)


def _attn_group(proj, gi, nb):
    _, dil = DILATED_GROUPS[gi]
    n = SEQ // dil
    cb = IN_WIDTH // HEAD_DIM
    p3 = proj.reshape(nb, n, dil * IN_WIDTH)
    slopes = jnp.asarray(ALIBI_SLOPES[gi * HEADS_PER_GROUP:(gi + 1) * HEADS_PER_GROUP], F32)
    slopes = jnp.broadcast_to(slopes[:, None, None], (HEADS_PER_GROUP, 1, 2 * ATTN_QB))
    h0 = gi * HEADS_PER_GROUP

    def col(off):
        return lambda b, r, h: (b, 0, r * cb + off // HEAD_DIM + h0 + h)

    out_map = lambda b, r, h: (b, 0, r * HEADS_PER_GROUP + h)
    o, lse = pl.pallas_call(
        functools.partial(_attn_kernel, n=n, dil=dil),
        grid=(nb, dil, HEADS_PER_GROUP),
        in_specs=[
            pl.BlockSpec((None, 1, 2 * ATTN_QB), lambda b, r, h: (h, 0, 0)),
            pl.BlockSpec((None, n, HEAD_DIM), col(Q_OFF)),
            pl.BlockSpec((None, n, HEAD_DIM), col(K_OFF)),
            pl.BlockSpec((None, n, HEAD_DIM), col(V_OFF)),
        ],
        out_specs=[
            pl.BlockSpec((None, n, HEAD_DIM), out_map),
            pl.BlockSpec((None, n, HEAD_DIM), out_map),
        ],
        out_shape=[
            jax.ShapeDtypeStruct((nb, n, dil * ATTN_OUT_WIDTH), BF16),
            jax.ShapeDtypeStruct((nb, n, dil * ATTN_OUT_WIDTH), F32),
        ],
        compiler_params=_params("parallel", "parallel", "parallel"),
        name=f"attn_g{gi}",
    )(slopes, p3, p3, p3)
    return o.reshape(nb * SEQ, ATTN_OUT_WIDTH), lse.reshape(nb * SEQ, ATTN_OUT_WIDTH)


def _shift_rows(x, k, row):
    n = x.shape[0]
    if k > 0:
        return jnp.where(row >= k, pltpu.roll(x, k, 0), 0.0)
    return jnp.where(row < n + k, pltpu.roll(x, n + k, 0), 0.0)


def _pool_kernel(u_ref, w_ref, s_ref, o_ref):
    n = u_ref.shape[0]
    row = lax.broadcasted_iota(jnp.int32, (n, POOL_GROUP), 0)
    for gi, win in enumerate(POOL_WINDOWS):
        half = win // 2
        cols = slice(gi * POOL_GROUP, (gi + 1) * POOL_GROUP)
        u = u_ref[:, cols].astype(F32)
        back = _shift_rows(u, 1, row)
        fwd = u
        step = 1
        while step < half:
            back = back + _shift_rows(back, step, row)
            fwd = fwd + _shift_rows(fwd, -step, row)
            step *= 2
        t = row[:, :1]
        lo = jnp.clip(t - half, 0, n - 1)
        hi = jnp.clip(t + half - 1, 0, n - 1)
        cnt = (hi - lo + 1).astype(F32)
        pooled = (back + fwd) / cnt - u
        y = jnp.dot(pooled.astype(BF16), w_ref[gi], preferred_element_type=F32)
        o_ref[:, cols] = (y * s_ref[:, cols]).astype(o_ref.dtype)


def _pool(proj, pool_w, pool_scale, nb):
    p3 = proj.reshape(nb, SEQ, IN_WIDTH)
    out = pl.pallas_call(
        _pool_kernel,
        grid=(nb,),
        in_specs=[
            pl.BlockSpec((None, SEQ, POOL_WIDTH), lambda b: (b, 0, POOL_OFF // POOL_WIDTH)),
            pl.BlockSpec((len(POOL_WINDOWS), POOL_GROUP, POOL_GROUP), lambda b: (0, 0, 0)),
            pl.BlockSpec((1, POOL_WIDTH), lambda b: (0, 0)),
        ],
        out_specs=pl.BlockSpec((None, SEQ, POOL_WIDTH), lambda b: (b, 0, 0)),
        out_shape=jax.ShapeDtypeStruct((nb, SEQ, POOL_WIDTH), BF16),
        compiler_params=_params("parallel"),
        name="pool",
    )(p3, pool_w.astype(BF16), pool_scale.reshape(1, POOL_WIDTH))
    return out.reshape(nb * SEQ, POOL_WIDTH)


def _s5_tables(a_re, a_im, log_step, b_re, b_im, c_re, c_im):
    t_len = SSM_CHUNK
    hp = lax.Precision.HIGHEST
    lr, li = a_re.astype(F32), a_im.astype(F32)
    dt = jnp.exp(log_step.astype(F32))[..., None]
    mag = jnp.exp(lr * dt)
    ab_r, ab_i = mag * jnp.cos(li * dt), mag * jnp.sin(li * dt)
    den = lr * lr + li * li
    xr, xi = ab_r - 1.0, ab_i
    co_r = (xr * lr + xi * li) / den
    co_i = (xi * lr - xr * li) / den
    br, bi = b_re.astype(F32), b_im.astype(F32)
    bb_r = co_r[..., None] * br - co_i[..., None] * bi
    bb_i = co_r[..., None] * bi + co_i[..., None] * br
    cr, ci = c_re.astype(F32), c_im.astype(F32)

    tau = jnp.arange(t_len + 1, dtype=F32)
    pmag = jnp.exp((lr * dt)[..., None] * tau)
    pang = (li * dt)[..., None] * tau
    pw_r, pw_i = pmag * jnp.cos(pang), pmag * jnp.sin(pang)

    cb_r = cr[:, :, :, None, :] * jnp.swapaxes(bb_r, 2, 3)[:, :, None, :, :] \
        - ci[:, :, :, None, :] * jnp.swapaxes(bb_i, 2, 3)[:, :, None, :, :]
    cb_i = cr[:, :, :, None, :] * jnp.swapaxes(bb_i, 2, 3)[:, :, None, :, :] \
        + ci[:, :, :, None, :] * jnp.swapaxes(bb_r, 2, 3)[:, :, None, :, :]
    kern = jnp.einsum('dgabp,dgpt->dgabt', cb_r, pw_r[..., :t_len], precision=hp) \
        - jnp.einsum('dgabp,dgpt->dgabt', cb_i, pw_i[..., :t_len], precision=hp)
    kf, kb = kern[0], kern[1]
    zero = jnp.zeros_like(kf[..., :1])
    kvec = jnp.concatenate([zero, kb[..., :0:-1], kf[..., :1] + kb[..., :1], kf[..., 1:]], axis=-1)
    kvec = jnp.swapaxes(kvec, 1, 2).reshape(SSM_GROUPS, SSM_GROUP * SSM_GROUP, 2 * t_len)

    def cmul(ar, ai, xr_, xi_):
        return ar * xr_ - ai * xi_, ar * xi_ + ai * xr_

    pf_r, pf_i = pw_r[0][..., t_len - 1::-1][..., :t_len], pw_i[0][..., t_len - 1::-1][..., :t_len]
    pb_r, pb_i = pw_r[1][..., :t_len], pw_i[1][..., :t_len]
    ef_r, ef_i = cmul(pf_r[:, :, None, :], pf_i[:, :, None, :], bb_r[0][..., None], bb_i[0][..., None])
    eb_r, eb_i = cmul(pb_r[:, :, None, :], pb_i[:, :, None, :], bb_r[1][..., None], bb_i[1][..., None])
    to_e = lambda a: jnp.transpose(a, (0, 2, 3, 1)).reshape(SSM_GROUPS, SSM_GROUP * t_len, SSM_STATE)
    e_tab = jnp.concatenate([to_e(ef_r), to_e(ef_i), to_e(eb_r), to_e(eb_i)], axis=-1)

    qf_r, qf_i = pw_r[0][..., 1:], pw_i[0][..., 1:]
    qb_r, qb_i = pw_r[1][..., :0:-1], pw_i[1][..., :0:-1]
    wf_r, wf_i = cmul(jnp.swapaxes(cr[0], 1, 2)[..., None], jnp.swapaxes(ci[0], 1, 2)[..., None],
                      qf_r[:, :, None, :], qf_i[:, :, None, :])
    wb_r, wb_i = cmul(jnp.swapaxes(cr[1], 1, 2)[..., None], jnp.swapaxes(ci[1], 1, 2)[..., None],
                      qb_r[:, :, None, :], qb_i[:, :, None, :])
    to_f = lambda a: a.reshape(SSM_GROUPS, SSM_STATE, SSM_GROUP * t_len)
    f_tab = jnp.concatenate([to_f(wf_r), to_f(-wf_i), to_f(wb_r), to_f(-wb_i)], axis=1)

    at_r, at_i = pw_r[..., t_len], pw_i[..., t_len]
    row0 = jnp.concatenate([at_r[0], at_r[0], at_r[1], at_r[1]], axis=-1)
    row1 = jnp.concatenate([-at_i[0], at_i[0], -at_i[1], at_i[1]], axis=-1)
    dec = jnp.concatenate([row0[:, None, :], row1[:, None, :], jnp.zeros((SSM_GROUPS, 6, 4 * SSM_STATE), F32)], axis=1)
    return kvec, e_tab.astype(BF16), f_tab.astype(BF16), dec


def _s5_kernel(u_ref, kvec_ref, e_ref, f_ref, dec_ref, y_ref, m_scr, hc_scr, *, nb):
    t_len = SSM_CHUNK
    h = SSM_GROUP

    def build(hi, carry):
        for ho in range(h):
            row = kvec_ref[pl.ds(hi * h + ho, 1), :]
            rolled = pltpu.roll(jnp.broadcast_to(row, (t_len, 2 * t_len)), 0, 1, stride=1, stride_axis=0)
            m_scr[pl.ds(pl.multiple_of(hi * t_len, t_len), t_len), ho * t_len:(ho + 1) * t_len] = (
                rolled[:, t_len:].astype(BF16))
        return carry

    lax.fori_loop(0, h, build, 0)

    x = jnp.concatenate([u_ref[hi] for hi in range(h)], axis=1)
    s = jnp.dot(x, e_ref[...], preferred_element_type=F32)

    d0, d1 = dec_ref[0:1, :], dec_ref[1:2, :]
    for half, order in ((0, range(N_CHUNKS)), (1, range(N_CHUNKS - 1, -1, -1))):
        lanes = slice(half * 2 * SSM_STATE, (half + 1) * 2 * SSM_STATE)
        a0, a1 = d0[:, lanes], d1[:, lanes]
        state = jnp.zeros((nb, 2 * SSM_STATE), F32)
        for c in order:
            rows = slice(c * nb, (c + 1) * nb)
            hc_scr[rows, lanes] = state
            state = a0 * state + a1 * pltpu.roll(state, SSM_STATE, 1) + s[rows, lanes]

    hc = hc_scr[...]
    hc_hi = hc.astype(BF16)
    hc_lo = (hc - hc_hi.astype(F32)).astype(BF16)
    for ho in range(0, h, 2):
        cols = slice(ho * t_len, (ho + 2) * t_len)
        y = jnp.dot(x, m_scr[:, cols], preferred_element_type=F32)
        y = y + jnp.dot(hc_hi, f_ref[:, cols], preferred_element_type=F32)
        y = y + jnp.dot(hc_lo, f_ref[:, cols], preferred_element_type=F32)
        y_ref[ho] = y[:, :t_len]
        y_ref[ho + 1] = y[:, t_len:]


def _s5_mix(u_t, kvec, e_tab, f_tab, dec, nb):
    rows = N_CHUNKS * nb
    tl = SSM_CHUNK
    return pl.pallas_call(
        functools.partial(_s5_kernel, nb=nb),
        grid=(SSM_GROUPS,),
        in_specs=[
            pl.BlockSpec((SSM_GROUP, rows, tl), lambda g: (g, 0, 0)),
            pl.BlockSpec((None, SSM_GROUP * SSM_GROUP, 2 * tl), lambda g: (g, 0, 0)),
            pl.BlockSpec((None, SSM_GROUP * tl, 4 * SSM_STATE), lambda g: (g, 0, 0)),
            pl.BlockSpec((None, 4 * SSM_STATE, SSM_GROUP * tl), lambda g: (g, 0, 0)),
            pl.BlockSpec((None, 8, 4 * SSM_STATE), lambda g: (g, 0, 0)),
        ],
        out_specs=pl.BlockSpec((SSM_GROUP, rows, tl), lambda g: (g, 0, 0)),
        out_shape=jax.ShapeDtypeStruct((SSM_WIDTH, rows, tl), F32),
        scratch_shapes=[
            pltpu.VMEM((SSM_GROUP * tl, SSM_GROUP * tl), BF16),
            pltpu.VMEM((rows, 4 * SSM_STATE), F32),
        ],
        compiler_params=_params("parallel"),
        name="s5_mix",
    )(u_t, kvec, e_tab, f_tab, dec)


def _glu_kernel(y_ref, u_ref, d_ref, w_ref, o_ref):
    y = y_ref[...] + d_ref[...] * u_ref[...].astype(F32)
    g = jax.nn.gelu(y)
    z = jnp.dot(w_ref[...], g.astype(BF16), preferred_element_type=F32)
    o_ref[...] = (g * jax.nn.sigmoid(z)).T.astype(o_ref.dtype)


def _s5_glu(y_t, u_t, d_skip, glu_w_t, nb):
    tokens = y_t.shape[1]
    tl = SSM_CHUNK
    return pl.pallas_call(
        _glu_kernel,
        grid=(tokens // tl,),
        in_specs=[
            pl.BlockSpec((SSM_WIDTH, tl), lambda i: (0, i)),
            pl.BlockSpec((SSM_WIDTH, tl), lambda i: (0, i)),
            pl.BlockSpec((SSM_WIDTH, 1), lambda i: (0, 0)),
            pl.BlockSpec((SSM_WIDTH, SSM_WIDTH), lambda i: (0, 0)),
        ],
        out_specs=pl.BlockSpec((tl, SSM_WIDTH), lambda i: ((i % nb) * N_CHUNKS + i // nb, 0)),
        out_shape=jax.ShapeDtypeStruct((tokens, SSM_WIDTH), BF16),
        compiler_params=_params("parallel"),
        name="s5_glu",
    )(y_t, u_t, d_skip.reshape(SSM_WIDTH, 1), glu_w_t)


def _merge_kernel(o1, o2, o3, l1, l2, l3, pool_ref, ssm_ref, ga, gp, gs, wa, wp, ws, out_ref, attn_scr):
    @pl.when(pl.program_id(1) == 0)
    def _():
        a1, a2, a3 = l1[...], l2[...], l3[...]
        m = jnp.maximum(jnp.maximum(a1, a2), a3)
        e1, e2, e3 = jnp.exp(a1 - m), jnp.exp(a2 - m), jnp.exp(a3 - m)
        inv = 1.0 / (e1 + e2 + e3)
        y = (o1[...].astype(F32) * (e1 * inv) + o2[...].astype(F32) * (e2 * inv)
             + o3[...].astype(F32) * (e3 * inv))
        attn_scr[...] = y.astype(BF16)

    ya = jnp.dot(attn_scr[...], wa[...], preferred_element_type=F32)
    yp = jnp.dot(pool_ref[...], wp[...], preferred_element_type=F32)
    ys = jnp.dot(ssm_ref[...], ws[...], preferred_element_type=F32)
    merged = (jax.nn.sigmoid(ga[...].astype(F32)) * ya + jax.nn.sigmoid(gp[...].astype(F32)) * yp
              + jax.nn.sigmoid(gs[...].astype(F32)) * ys)
    out_ref[...] = merged.astype(out_ref.dtype)


def _merge(os_, ls_, pool_y, ssm_y, proj, wa, wp, ws, tm=512, tn=512):
    m = proj.shape[0]
    gb = GATE_OFF // tn
    nj = D_MODEL // tn
    row = lambda i, j: (i, 0)
    return pl.pallas_call(
        _merge_kernel,
        grid=(m // tm, nj),
        in_specs=[pl.BlockSpec((tm, ATTN_OUT_WIDTH), row)] * 6 + [
            pl.BlockSpec((tm, POOL_WIDTH), row),
            pl.BlockSpec((tm, SSM_WIDTH), row),
            pl.BlockSpec((tm, tn), lambda i, j: (i, gb + j)),
            pl.BlockSpec((tm, tn), lambda i, j: (i, gb + nj + j)),
            pl.BlockSpec((tm, tn), lambda i, j: (i, gb + 2 * nj + j)),
            pl.BlockSpec((ATTN_OUT_WIDTH, tn), lambda i, j: (0, j)),
            pl.BlockSpec((POOL_WIDTH, tn), lambda i, j: (0, j)),
            pl.BlockSpec((SSM_WIDTH, tn), lambda i, j: (0, j)),
        ],
        out_specs=pl.BlockSpec((tm, tn), lambda i, j: (i, j)),
        out_shape=jax.ShapeDtypeStruct((m, D_MODEL), BF16),
        scratch_shapes=[pltpu.VMEM((tm, ATTN_OUT_WIDTH), BF16)],
        compiler_params=_params("parallel", "arbitrary"),
        name="merge",
    )(*os_, *ls_, pool_y, ssm_y, proj, proj, proj, wa, wp, ws)


def _outproj_kernel(mg_ref, x_ref, wo_ref, g_ref, rw_ref, xo_ref, h_ref, aff_ref):
    xn = x_ref[...] + jnp.dot(mg_ref[...], wo_ref[...], preferred_element_type=F32)
    xo_ref[...] = xn
    ms = jnp.mean(xn * xn, axis=-1, keepdims=True)
    h = xn * lax.rsqrt(ms + EPS) * g_ref[...]
    h_ref[...] = h.astype(h_ref.dtype)
    logits = lax.dot_general(rw_ref[...], h, (((1,), (1,)), ((), ())), preferred_element_type=F32,
                             precision=lax.Precision.HIGHEST)
    mx = jnp.max(logits, axis=0, keepdims=True)
    ex = jnp.exp(logits - mx)
    aff_ref[...] = ex / jnp.sum(ex, axis=0, keepdims=True)


def _outproj(merged, x, wo, g2, router_w_t, tm=512):
    m = x.shape[0]
    return pl.pallas_call(
        _outproj_kernel,
        grid=(m // tm,),
        in_specs=[
            pl.BlockSpec((tm, D_MODEL), lambda i: (i, 0)),
            pl.BlockSpec((tm, D_MODEL), lambda i: (i, 0)),
            pl.BlockSpec((D_MODEL, D_MODEL), lambda i: (0, 0)),
            pl.BlockSpec((1, D_MODEL), lambda i: (0, 0)),
            pl.BlockSpec((N_EXPERTS, D_MODEL), lambda i: (0, 0)),
        ],
        out_specs=[
            pl.BlockSpec((tm, D_MODEL), lambda i: (i, 0)),
            pl.BlockSpec((tm, D_MODEL), lambda i: (i, 0)),
            pl.BlockSpec((N_EXPERTS, tm), lambda i: (0, i)),
        ],
        out_shape=[
            jax.ShapeDtypeStruct((m, D_MODEL), F32),
            jax.ShapeDtypeStruct((m, D_MODEL), BF16),
            jax.ShapeDtypeStruct((N_EXPERTS, m), F32),
        ],
        compiler_params=_params("parallel"),
        name="outproj",
    )(merged, x, wo, g2.reshape(1, D_MODEL), router_w_t)


def _ffn_kernel(x_ref, gate_ref, wg_ref, wu_ref, wd_ref, o_ref, acc_ref):
    f = pl.program_id(2)

    @pl.when(f == 0)
    def _():
        acc_ref[...] = jnp.zeros_like(acc_ref)

    x = x_ref[...]
    hg = jnp.dot(x, wg_ref[...], preferred_element_type=F32)
    hu = jnp.dot(x, wu_ref[...], preferred_element_type=F32)
    hid = (jax.nn.silu(hg) * hu).astype(BF16)
    acc_ref[...] += jnp.dot(hid, wd_ref[...], preferred_element_type=F32)

    @pl.when(f == pl.num_programs(2) - 1)
    def _():
        o_ref[...] = (acc_ref[...] * gate_ref[...]).astype(o_ref.dtype)


def _ffn(xin, gate, wg, wu, wd, tm=1024, tf=256):
    e, cap, d = xin.shape
    ff = wg.shape[2]
    tm = min(tm, cap)
    return pl.pallas_call(
        _ffn_kernel,
        grid=(e, cap // tm, ff // tf),
        in_specs=[
            pl.BlockSpec((None, tm, d), lambda ei, i, f: (ei, i, 0)),
            pl.BlockSpec((None, tm, 1), lambda ei, i, f: (ei, i, 0)),
            pl.BlockSpec((None, d, tf), lambda ei, i, f: (ei, 0, f)),
            pl.BlockSpec((None, d, tf), lambda ei, i, f: (ei, 0, f)),
            pl.BlockSpec((None, tf, d), lambda ei, i, f: (ei, f, 0)),
        ],
        out_specs=pl.BlockSpec((None, tm, d), lambda ei, i, f: (ei, i, 0)),
        out_shape=jax.ShapeDtypeStruct((e, cap, d), F32),
        scratch_shapes=[pltpu.VMEM((tm, d), F32)],
        compiler_params=_params("parallel", "parallel", "arbitrary"),
        name="expert_ffn",
    )(xin, gate[..., None], wg, wu, wd)


def _final_norm_kernel(x_ref, g_ref, o_ref):
    x = x_ref[...]
    ms = jnp.mean(x * x, axis=-1, keepdims=True)
    o_ref[...] = x * lax.rsqrt(ms + EPS) * g_ref[...]


def _final_norm(x, g, tm=1024):
    m, d = x.shape
    return pl.pallas_call(
        _final_norm_kernel,
        grid=(m // tm,),
        in_specs=[pl.BlockSpec((tm, d), lambda i: (i, 0)), pl.BlockSpec((1, d), lambda i: (0, 0))],
        out_specs=pl.BlockSpec((tm, d), lambda i: (i, 0)),
        out_shape=jax.ShapeDtypeStruct((m, d), F32),
        compiler_params=_params("parallel"),
        name="final_norm",
    )(x, g.reshape(1, d))


def _mixer(x, nb, norm_g, w_in_p, pool_w, pool_scale, tables, d_skip, glu_w_t, wa, wp, ws, wo, norm2_g, router_w_t):
    proj = _inproj(x, norm_g, w_in_p)
    attn = [_attn_group(proj, gi, nb) for gi in range(len(DILATED_GROUPS))]
    pool_y = _pool(proj, pool_w, pool_scale, nb)

    u = proj[:, SSM_OFF:SSM_OFF + SSM_WIDTH].reshape(nb, N_CHUNKS, SSM_CHUNK, SSM_WIDTH)
    u_t = jnp.transpose(u, (3, 1, 0, 2)).reshape(SSM_WIDTH, N_CHUNKS * nb, SSM_CHUNK)
    y_t = _s5_mix(u_t, *tables, nb)
    ssm_y = _s5_glu(y_t.reshape(SSM_WIDTH, -1), u_t.reshape(SSM_WIDTH, -1), d_skip, glu_w_t, nb)

    merged = _merge([a[0] for a in attn], [a[1] for a in attn], pool_y, ssm_y, proj, wa, wp, ws)
    return _outproj(merged, x, wo, norm2_g, router_w_t)


def _experts(x, h, aff_t, wg, wu, wd):
    t = x.shape[0]
    cap = CAPACITY_FACTOR * t // N_EXPERTS
    gate, idx = lax.top_k(aff_t, cap)
    xin = h[idx]
    out = _ffn(xin, gate, wg, wu, wd)
    return x.at[idx.reshape(-1)].add(out.reshape(-1, D_MODEL))


def kernel(x_prompt, x_sample, norm1_g, w_in, pool_w, pool_scale, ssm_a_re, ssm_a_im, ssm_log_step, ssm_b_re,
           ssm_b_im, ssm_c_re, ssm_c_im, ssm_d, ssm_glu_w, w_br_attn, w_br_pool, w_br_ssm, w_out, norm2_g,
           router_w, w_gate, w_up, w_down, final_norm_g):
    nbp, nbs = x_prompt.shape[0], x_sample.shape[0]
    nb = nbp + nbs
    tp = nbp * SEQ
    x = jnp.concatenate([x_prompt, x_sample], axis=0).reshape(nb * SEQ, D_MODEL)
    for l in range(DEPTH):
        w_in_p = jnp.concatenate(
            [w_in[l][:, 3 * ATTN_WIDTH:3 * ATTN_WIDTH + POOL_WIDTH + SSM_WIDTH], w_in[l][:, :3 * ATTN_WIDTH],
             w_in[l][:, 3 * ATTN_WIDTH + POOL_WIDTH + SSM_WIDTH:]], axis=1).astype(BF16)
        tables = _s5_tables(ssm_a_re[l], ssm_a_im[l], ssm_log_step[l], ssm_b_re[l], ssm_b_im[l], ssm_c_re[l],
                            ssm_c_im[l])
        x, h, aff_t = _mixer(
            x, nb, norm1_g[l], w_in_p, pool_w[l], pool_scale[l], tables, ssm_d[l], ssm_glu_w[l].T.astype(BF16),
            w_br_attn[l].astype(BF16), w_br_pool[l].astype(BF16), w_br_ssm[l].astype(BF16), w_out[l].astype(BF16),
            norm2_g[l], router_w[l].T)
        wg, wu, wd = w_gate[l].astype(BF16), w_up[l].astype(BF16), w_down[l].astype(BF16)
        xp = _experts(x[:tp], h[:tp], aff_t[:, :tp], wg, wu, wd)
        xs = _experts(x[tp:], h[tp:], aff_t[:, tp:], wg, wu, wd)
        x = jnp.concatenate([xp, xs], axis=0)
    y = _final_norm(x, final_norm_g).reshape(nb, SEQ, D_MODEL)
    return (y[:nbp], y[nbp:])
```

```python
import functools
import math

import jax
import jax.numpy as jnp
from jax import lax
from jax.experimental import pallas as pl
from jax.experimental.pallas import tpu as pltpu

F32 = jnp.float32
BF16 = jnp.bfloat16

D_MODEL = 2048
SEQ = 2048
DEPTH = 4
EPS = 1e-6
HEAD_DIM = 128
HEADS_PER_GROUP = 4
DILATED_GROUPS = ((128, 1), (512, 4), (2048, 16))
N_ATTN_HEADS = HEADS_PER_GROUP * len(DILATED_GROUPS)
ATTN_WIDTH = N_ATTN_HEADS * HEAD_DIM
ATTN_OUT_WIDTH = HEADS_PER_GROUP * HEAD_DIM
ALIBI_SLOPES = tuple(2.0 ** (-8.0 * (h + 1) / N_ATTN_HEADS) for h in range(N_ATTN_HEADS))
POOL_WINDOWS = (2, 4, 8, 16)
POOL_WIDTH = D_MODEL // 2
POOL_GROUP = POOL_WIDTH // len(POOL_WINDOWS)
SSM_WIDTH = D_MODEL // 2
SSM_GROUP = 16
SSM_GROUPS = SSM_WIDTH // SSM_GROUP
SSM_STATE = 64
IN_WIDTH = 3 * ATTN_WIDTH + POOL_WIDTH + SSM_WIDTH + 3 * D_MODEL
N_EXPERTS = 16
EXPERT_FF = 11 * D_MODEL // 8
CAPACITY_FACTOR = 2

POOL_OFF = 0
SSM_OFF = POOL_OFF + POOL_WIDTH
Q_OFF = SSM_OFF + SSM_WIDTH
K_OFF = Q_OFF + ATTN_OUT_WIDTH
V_OFF = K_OFF + ATTN_OUT_WIDTH
GATE_OFF = V_OFF + ATTN_OUT_WIDTH
MAIN_WIDTH = GATE_OFF + 3 * D_MODEL
QKV_WIDTH = 3 * ATTN_OUT_WIDTH

LANE = 128
SSM_CHUNK = LANE
N_CHUNKS = SEQ // SSM_CHUNK
ATTN_QB = 128
ATTN_HALF = 64
VMEM_LIMIT = 48 * 1024 * 1024


def _params(*sem):
    return pltpu.CompilerParams(dimension_semantics=sem, vmem_limit_bytes=VMEM_LIMIT)


def _inproj_kernel(x_ref, g_ref, w_ref, o_ref, h_ref):
    @pl.when(pl.program_id(1) == 0)
    def _():
        x = x_ref[...]
        ms = jnp.mean(x * x, axis=-1, keepdims=True)
        h_ref[...] = (x * lax.rsqrt(ms + EPS) * g_ref[...]).astype(BF16)

    o_ref[...] = jnp.dot(h_ref[...], w_ref[...], preferred_element_type=F32).astype(o_ref.dtype)


def _inproj(x, g, w, tm=1024, tn=512):
    m, d = x.shape
    n = w.shape[1]
    return pl.pallas_call(
        _inproj_kernel,
        grid=(m // tm, n // tn),
        in_specs=[
            pl.BlockSpec((tm, d), lambda i, j: (i, 0)),
            pl.BlockSpec((1, d), lambda i, j: (0, 0)),
            pl.BlockSpec((d, tn), lambda i, j: (0, j)),
        ],
        out_specs=[pl.BlockSpec((tm, tn), lambda i, j: (i, j)), pl.BlockSpec((tm, d), lambda i, j: (i, 0))],
        out_shape=[jax.ShapeDtypeStruct((m, n), BF16), jax.ShapeDtypeStruct((m, d), BF16)],
        compiler_params=_params("parallel", "arbitrary"),
        name="inproj",
    )(x, g.reshape(1, d), w)


def _dilproj_kernel(h_ref, w_ref, o_ref, y_scr, *, dil):
    rows = o_ref.shape[0]
    y = jnp.dot(h_ref[...], w_ref[...], preferred_element_type=F32)
    for c in range(QKV_WIDTH // LANE):
        y_scr[c] = y[:, c * LANE:(c + 1) * LANE]
    for r in range(dil):
        for c in range(QKV_WIDTH // LANE):
            col = r * QKV_WIDTH + c * LANE
            o_ref[:, col:col + LANE] = y_scr[c, pl.ds(r, rows, stride=dil), :].astype(o_ref.dtype)


def _dilproj(h, w, dil, tm=512):
    m, d = h.shape
    return pl.pallas_call(
        functools.partial(_dilproj_kernel, dil=dil),
        grid=(m // tm,),
        in_specs=[pl.BlockSpec((tm, d), lambda i: (i, 0)), pl.BlockSpec((d, QKV_WIDTH), lambda i: (0, 0))],
        out_specs=pl.BlockSpec((tm // dil, dil * QKV_WIDTH), lambda i: (i, 0)),
        out_shape=jax.ShapeDtypeStruct((m // dil, dil * QKV_WIDTH), BF16),
        scratch_shapes=[pltpu.VMEM((QKV_WIDTH // LANE, tm, LANE), F32)],
        compiler_params=_params("parallel"),
        name=f"dilproj_d{dil}",
    )(h, w)


def _attn_kernel(slope_ref, q_ref, k_ref, v_ref, o_ref, l_ref, *, n, dil):
    kw = min(2 * ATTN_QB, n)
    nqb = n // ATTN_QB
    scale = HEAD_DIM ** -0.5

    def body(j, carry):
        qs = pl.multiple_of(j * ATTN_QB, ATTN_QB)
        ks = pl.multiple_of(jnp.clip(qs - ATTN_HALF, 0, n - kw), ATTN_HALF)
        qpos = qs + lax.broadcasted_iota(jnp.int32, (ATTN_QB, kw), 0)
        kpos = ks + lax.broadcasted_iota(jnp.int32, (ATTN_QB, kw), 1)
        rel = jnp.abs(kpos - qpos)
        valid = rel <= ATTN_HALF
        relf = rel.astype(F32)
        for h in range(HEADS_PER_GROUP):
            cols = slice(h * HEAD_DIM, (h + 1) * HEAD_DIM)
            q = q_ref[pl.ds(qs, ATTN_QB), cols]
            k = k_ref[pl.ds(ks, kw), cols]
            v = v_ref[pl.ds(ks, kw), cols]
            s = lax.dot_general(q, k, (((1,), (1,)), ((), ())), preferred_element_type=F32) * scale
            neg_slope = -slope_ref[h][:, :kw] * float(dil)
            s = jnp.where(valid, s + neg_slope * relf, -jnp.inf)
            m = jnp.max(s, axis=1, keepdims=True)
            p = jnp.exp(s - m)
            l = jnp.sum(p, axis=1, keepdims=True)
            o = jnp.dot(p.astype(BF16), v, preferred_element_type=F32) / l
            o_ref[pl.ds(qs, ATTN_QB), cols] = o.astype(o_ref.dtype)
            l_ref[pl.ds(qs, ATTN_QB), cols] = jnp.broadcast_to(m + jnp.log(l), (ATTN_QB, HEAD_DIM))
        return carry

    lax.fori_loop(0, nqb, body, 0)


def _attn_group(qkv, gi, nb, col0):
    _, dil = DILATED_GROUPS[gi]
    n = SEQ // dil
    aw = ATTN_OUT_WIDTH
    cb = qkv.shape[2] // dil // aw
    slopes = jnp.asarray(ALIBI_SLOPES[gi * HEADS_PER_GROUP:(gi + 1) * HEADS_PER_GROUP], F32)
    slopes = jnp.broadcast_to(slopes[:, None, None], (HEADS_PER_GROUP, 1, 2 * ATTN_QB))

    def col(which):
        return lambda b, r: (b, 0, r * cb + col0 // aw + which)

    out_map = lambda b, r: (b, 0, r)
    return pl.pallas_call(
        functools.partial(_attn_kernel, n=n, dil=dil),
        grid=(nb, dil),
        in_specs=[
            pl.BlockSpec((HEADS_PER_GROUP, 1, 2 * ATTN_QB), lambda b, r: (0, 0, 0)),
            pl.BlockSpec((None, n, aw), col(0)),
            pl.BlockSpec((None, n, aw), col(1)),
            pl.BlockSpec((None, n, aw), col(2)),
        ],
        out_specs=[pl.BlockSpec((None, n, aw), out_map), pl.BlockSpec((None, n, aw), out_map)],
        out_shape=[
            jax.ShapeDtypeStruct((nb, n, dil * aw), BF16),
            jax.ShapeDtypeStruct((nb, n, dil * aw), F32),
        ],
        compiler_params=_params("parallel", "parallel"),
        name=f"attn_g{gi}",
    )(slopes, qkv, qkv, qkv)


def _shift_rows(x, k, row):
    n = x.shape[0]
    if k > 0:
        return jnp.where(row >= k, pltpu.roll(x, k, 0), 0.0)
    return jnp.where(row < n + k, pltpu.roll(x, n + k, 0), 0.0)


def _pool_kernel(u_ref, w_ref, s_ref, o_ref):
    n = u_ref.shape[0]
    row = lax.broadcasted_iota(jnp.int32, (n, POOL_GROUP), 0)
    for gi, win in enumerate(POOL_WINDOWS):
        half = win // 2
        cols = slice(gi * POOL_GROUP, (gi + 1) * POOL_GROUP)
        u = u_ref[:, cols].astype(F32)
        back = _shift_rows(u, 1, row)
        fwd = u
        step = 1
        while step < half:
            back = back + _shift_rows(back, step, row)
            fwd = fwd + _shift_rows(fwd, -step, row)
            step *= 2
        t = row[:, :1]
        lo = jnp.clip(t - half, 0, n - 1)
        hi = jnp.clip(t + half - 1, 0, n - 1)
        cnt = (hi - lo + 1).astype(F32)
        pooled = (back + fwd) / cnt - u
        y = jnp.dot(pooled.astype(BF16), w_ref[gi], preferred_element_type=F32)
        o_ref[:, cols] = (y * s_ref[:, cols]).astype(o_ref.dtype)


def _pool(proj, pool_w, pool_scale, nb):
    p3 = proj.reshape(nb, SEQ, MAIN_WIDTH)
    out = pl.pallas_call(
        _pool_kernel,
        grid=(nb,),
        in_specs=[
            pl.BlockSpec((None, SEQ, POOL_WIDTH), lambda b: (b, 0, POOL_OFF // POOL_WIDTH)),
            pl.BlockSpec((len(POOL_WINDOWS), POOL_GROUP, POOL_GROUP), lambda b: (0, 0, 0)),
            pl.BlockSpec((1, POOL_WIDTH), lambda b: (0, 0)),
        ],
        out_specs=pl.BlockSpec((None, SEQ, POOL_WIDTH), lambda b: (b, 0, 0)),
        out_shape=jax.ShapeDtypeStruct((nb, SEQ, POOL_WIDTH), BF16),
        compiler_params=_params("parallel"),
        name="pool",
    )(p3, pool_w.astype(BF16), pool_scale.reshape(1, POOL_WIDTH))
    return out.reshape(nb * SEQ, POOL_WIDTH)


def _s5_tables(a_re, a_im, log_step, b_re, b_im, c_re, c_im):
    t_len = SSM_CHUNK
    hp = lax.Precision.HIGHEST
    lr, li = a_re.astype(F32), a_im.astype(F32)
    dt = jnp.exp(log_step.astype(F32))[..., None]
    mag = jnp.exp(lr * dt)
    ab_r, ab_i = mag * jnp.cos(li * dt), mag * jnp.sin(li * dt)
    den = lr * lr + li * li
    xr, xi = ab_r - 1.0, ab_i
    co_r = (xr * lr + xi * li) / den
    co_i = (xi * lr - xr * li) / den
    br, bi = b_re.astype(F32), b_im.astype(F32)
    bb_r = co_r[..., None] * br - co_i[..., None] * bi
    bb_i = co_r[..., None] * bi + co_i[..., None] * br
    cr, ci = c_re.astype(F32), c_im.astype(F32)

    tau = jnp.arange(t_len + 1, dtype=F32)
    pmag = jnp.exp((lr * dt)[..., None] * tau)
    pang = (li * dt)[..., None] * tau
    pw_r, pw_i = pmag * jnp.cos(pang), pmag * jnp.sin(pang)

    cb_r = cr[:, :, :, None, :] * jnp.swapaxes(bb_r, 2, 3)[:, :, None, :, :] \
        - ci[:, :, :, None, :] * jnp.swapaxes(bb_i, 2, 3)[:, :, None, :, :]
    cb_i = cr[:, :, :, None, :] * jnp.swapaxes(bb_i, 2, 3)[:, :, None, :, :] \
        + ci[:, :, :, None, :] * jnp.swapaxes(bb_r, 2, 3)[:, :, None, :, :]
    kern = jnp.einsum('dgabp,dgpt->dgabt', cb_r, pw_r[..., :t_len], precision=hp) \
        - jnp.einsum('dgabp,dgpt->dgabt', cb_i, pw_i[..., :t_len], precision=hp)
    kf, kb = kern[0], kern[1]
    zero = jnp.zeros_like(kf[..., :1])
    kvec = jnp.concatenate([zero, kb[..., :0:-1], kf[..., :1] + kb[..., :1], kf[..., 1:]], axis=-1)
    kvec = jnp.swapaxes(kvec, 1, 2).reshape(SSM_GROUPS, SSM_GROUP * SSM_GROUP, 2 * t_len)

    def cmul(ar, ai, xr_, xi_):
        return ar * xr_ - ai * xi_, ar * xi_ + ai * xr_

    pf_r, pf_i = pw_r[0][..., t_len - 1::-1][..., :t_len], pw_i[0][..., t_len - 1::-1][..., :t_len]
    pb_r, pb_i = pw_r[1][..., :t_len], pw_i[1][..., :t_len]
    ef_r, ef_i = cmul(pf_r[:, :, None, :], pf_i[:, :, None, :], bb_r[0][..., None], bb_i[0][..., None])
    eb_r, eb_i = cmul(pb_r[:, :, None, :], pb_i[:, :, None, :], bb_r[1][..., None], bb_i[1][..., None])
    to_e = lambda a: jnp.transpose(a, (0, 2, 3, 1)).reshape(SSM_GROUPS, SSM_GROUP * t_len, SSM_STATE)
    e_tab = jnp.concatenate([to_e(ef_r), to_e(ef_i), to_e(eb_r), to_e(eb_i)], axis=-1)

    qf_r, qf_i = pw_r[0][..., 1:], pw_i[0][..., 1:]
    qb_r, qb_i = pw_r[1][..., :0:-1], pw_i[1][..., :0:-1]
    wf_r, wf_i = cmul(jnp.swapaxes(cr[0], 1, 2)[..., None], jnp.swapaxes(ci[0], 1, 2)[..., None],
                      qf_r[:, :, None, :], qf_i[:, :, None, :])
    wb_r, wb_i = cmul(jnp.swapaxes(cr[1], 1, 2)[..., None], jnp.swapaxes(ci[1], 1, 2)[..., None],
                      qb_r[:, :, None, :], qb_i[:, :, None, :])
    to_f = lambda a: a.reshape(SSM_GROUPS, SSM_STATE, SSM_GROUP * t_len)
    f_tab = jnp.concatenate([to_f(wf_r), to_f(-wf_i), to_f(wb_r), to_f(-wb_i)], axis=1)

    at_r, at_i = pw_r[..., t_len], pw_i[..., t_len]
    row0 = jnp.concatenate([at_r[0], at_r[0], at_r[1], at_r[1]], axis=-1)
    row1 = jnp.concatenate([-at_i[0], at_i[0], -at_i[1], at_i[1]], axis=-1)
    dec = jnp.concatenate([row0[:, None, :], row1[:, None, :], jnp.zeros((SSM_GROUPS, 6, 4 * SSM_STATE), F32)], axis=1)
    return kvec, e_tab.astype(BF16), f_tab.astype(BF16), dec


def _s5_kernel(u_ref, kvec_ref, e_ref, f_ref, dec_ref, y_ref, m_scr, hc_scr, *, nb):
    t_len = SSM_CHUNK
    h = SSM_GROUP

    def build(hi, carry):
        for ho in range(h):
            row = kvec_ref[pl.ds(hi * h + ho, 1), :]
            rolled = pltpu.roll(jnp.broadcast_to(row, (t_len, 2 * t_len)), 0, 1, stride=1, stride_axis=0)
            m_scr[pl.ds(pl.multiple_of(hi * t_len, t_len), t_len), ho * t_len:(ho + 1) * t_len] = (
                rolled[:, t_len:].astype(BF16))
        return carry

    lax.fori_loop(0, h, build, 0)

    x = jnp.concatenate([u_ref[hi] for hi in range(h)], axis=1)
    s = jnp.dot(x, e_ref[...], preferred_element_type=F32)

    d0, d1 = dec_ref[0:1, :], dec_ref[1:2, :]
    for half, order in ((0, range(N_CHUNKS)), (1, range(N_CHUNKS - 1, -1, -1))):
        lanes = slice(half * 2 * SSM_STATE, (half + 1) * 2 * SSM_STATE)
        a0, a1 = d0[:, lanes], d1[:, lanes]
        state = jnp.zeros((nb, 2 * SSM_STATE), F32)
        for c in order:
            rows = slice(c * nb, (c + 1) * nb)
            hc_scr[rows, lanes] = state
            state = a0 * state + a1 * pltpu.roll(state, SSM_STATE, 1) + s[rows, lanes]

    hc = hc_scr[...]
    hc_hi = hc.astype(BF16)
    hc_lo = (hc - hc_hi.astype(F32)).astype(BF16)
    for ho in range(0, h, 2):
        cols = slice(ho * t_len, (ho + 2) * t_len)
        y = jnp.dot(x, m_scr[:, cols], preferred_element_type=F32)
        y = y + jnp.dot(hc_hi, f_ref[:, cols], preferred_element_type=F32)
        y = y + jnp.dot(hc_lo, f_ref[:, cols], preferred_element_type=F32)
        y_ref[ho] = y[:, :t_len]
        y_ref[ho + 1] = y[:, t_len:]


def _s5_mix(u_t, kvec, e_tab, f_tab, dec, nb):
    rows = N_CHUNKS * nb
    tl = SSM_CHUNK
    return pl.pallas_call(
        functools.partial(_s5_kernel, nb=nb),
        grid=(SSM_GROUPS,),
        in_specs=[
            pl.BlockSpec((SSM_GROUP, rows, tl), lambda g: (g, 0, 0)),
            pl.BlockSpec((None, SSM_GROUP * SSM_GROUP, 2 * tl), lambda g: (g, 0, 0)),
            pl.BlockSpec((None, SSM_GROUP * tl, 4 * SSM_STATE), lambda g: (g, 0, 0)),
            pl.BlockSpec((None, 4 * SSM_STATE, SSM_GROUP * tl), lambda g: (g, 0, 0)),
            pl.BlockSpec((None, 8, 4 * SSM_STATE), lambda g: (g, 0, 0)),
        ],
        out_specs=pl.BlockSpec((SSM_GROUP, rows, tl), lambda g: (g, 0, 0)),
        out_shape=jax.ShapeDtypeStruct((SSM_WIDTH, rows, tl), F32),
        scratch_shapes=[
            pltpu.VMEM((SSM_GROUP * tl, SSM_GROUP * tl), BF16),
            pltpu.VMEM((rows, 4 * SSM_STATE), F32),
        ],
        compiler_params=_params("parallel"),
        name="s5_mix",
    )(u_t, kvec, e_tab, f_tab, dec)


def _glu_kernel(y_ref, u_ref, d_ref, w_ref, o_ref):
    y = y_ref[...] + d_ref[...] * u_ref[...].astype(F32)
    g = jax.nn.gelu(y)
    z = jnp.dot(w_ref[...], g.astype(BF16), preferred_element_type=F32)
    o_ref[...] = (g * jax.nn.sigmoid(z)).T.astype(o_ref.dtype)


def _s5_glu(y_t, u_t, d_skip, glu_w_t, nb):
    tokens = y_t.shape[1]
    tl = SSM_CHUNK
    return pl.pallas_call(
        _glu_kernel,
        grid=(tokens // tl,),
        in_specs=[
            pl.BlockSpec((SSM_WIDTH, tl), lambda i: (0, i)),
            pl.BlockSpec((SSM_WIDTH, tl), lambda i: (0, i)),
            pl.BlockSpec((SSM_WIDTH, 1), lambda i: (0, 0)),
            pl.BlockSpec((SSM_WIDTH, SSM_WIDTH), lambda i: (0, 0)),
        ],
        out_specs=pl.BlockSpec((tl, SSM_WIDTH), lambda i: ((i % nb) * N_CHUNKS + i // nb, 0)),
        out_shape=jax.ShapeDtypeStruct((tokens, SSM_WIDTH), BF16),
        compiler_params=_params("parallel"),
        name="s5_glu",
    )(y_t, u_t, d_skip.reshape(SSM_WIDTH, 1), glu_w_t)


def _merge_kernel(o1, o2, o3, l1, l2, l3, pool_ref, ssm_ref, ga, gp, gs, wa, wp, ws, out_ref, attn_scr, os_scr, ls_scr):
    @pl.when(pl.program_id(1) == 0)
    def _():
        aw = ATTN_OUT_WIDTH
        for gi, (o_ref, l_ref) in enumerate(((o2, l2), (o3, l3))):
            dil = DILATED_GROUPS[gi + 1][1]
            rows = o_ref.shape[0]
            for r in range(dil):
                for c in range(HEADS_PER_GROUP):
                    col = r * aw + c * HEAD_DIM
                    os_scr[gi, c, pl.ds(r, rows, stride=dil), :] = o_ref[:, col:col + HEAD_DIM].astype(F32)
                    ls_scr[gi, c, pl.ds(r, rows, stride=dil), :] = l_ref[:, col:col + HEAD_DIM]
        for c in range(HEADS_PER_GROUP):
            cols = slice(c * HEAD_DIM, (c + 1) * HEAD_DIM)
            a1, a2, a3 = l1[:, cols], ls_scr[0, c], ls_scr[1, c]
            m = jnp.maximum(jnp.maximum(a1, a2), a3)
            e1, e2, e3 = jnp.exp(a1 - m), jnp.exp(a2 - m), jnp.exp(a3 - m)
            inv = 1.0 / (e1 + e2 + e3)
            y = o1[:, cols].astype(F32) * (e1 * inv) + os_scr[0, c] * (e2 * inv) + os_scr[1, c] * (e3 * inv)
            attn_scr[:, cols] = y.astype(BF16)

    ya = jnp.dot(attn_scr[...], wa[...], preferred_element_type=F32)
    yp = jnp.dot(pool_ref[...], wp[...], preferred_element_type=F32)
    ys = jnp.dot(ssm_ref[...], ws[...], preferred_element_type=F32)
    merged = (jax.nn.sigmoid(ga[...].astype(F32)) * ya + jax.nn.sigmoid(gp[...].astype(F32)) * yp
              + jax.nn.sigmoid(gs[...].astype(F32)) * ys)
    out_ref[...] = merged.astype(out_ref.dtype)


def _merge(os_, ls_, pool_y, ssm_y, proj, wa, wp, ws, tm=512, tn=512):
    m = proj.shape[0]
    gb = GATE_OFF // tn
    nj = D_MODEL // tn
    aw = ATTN_OUT_WIDTH
    row = lambda i, j: (i, 0)
    dils = [d for _, d in DILATED_GROUPS]
    attn_specs = [pl.BlockSpec((tm // d, d * aw), row) for d in dils]
    return pl.pallas_call(
        _merge_kernel,
        grid=(m // tm, nj),
        in_specs=attn_specs + attn_specs + [
            pl.BlockSpec((tm, POOL_WIDTH), row),
            pl.BlockSpec((tm, SSM_WIDTH), row),
            pl.BlockSpec((tm, tn), lambda i, j: (i, gb + j)),
            pl.BlockSpec((tm, tn), lambda i, j: (i, gb + nj + j)),
            pl.BlockSpec((tm, tn), lambda i, j: (i, gb + 2 * nj + j)),
            pl.BlockSpec((ATTN_OUT_WIDTH, tn), lambda i, j: (0, j)),
            pl.BlockSpec((POOL_WIDTH, tn), lambda i, j: (0, j)),
            pl.BlockSpec((SSM_WIDTH, tn), lambda i, j: (0, j)),
        ],
        out_specs=pl.BlockSpec((tm, tn), lambda i, j: (i, j)),
        out_shape=jax.ShapeDtypeStruct((m, D_MODEL), BF16),
        scratch_shapes=[
            pltpu.VMEM((tm, aw), BF16),
            pltpu.VMEM((len(dils) - 1, HEADS_PER_GROUP, tm, HEAD_DIM), F32),
            pltpu.VMEM((len(dils) - 1, HEADS_PER_GROUP, tm, HEAD_DIM), F32),
        ],
        compiler_params=_params("parallel", "arbitrary"),
        name="merge",
    )(*os_, *ls_, pool_y, ssm_y, proj, proj, proj, wa, wp, ws)


def _outproj_kernel(mg_ref, x_ref, wo_ref, g_ref, rw_ref, xo_ref, h_ref, aff_ref):
    xn = x_ref[...] + jnp.dot(mg_ref[...], wo_ref[...], preferred_element_type=F32)
    xo_ref[...] = xn
    ms = jnp.mean(xn * xn, axis=-1, keepdims=True)
    h = xn * lax.rsqrt(ms + EPS) * g_ref[...]
    h_ref[...] = h.astype(h_ref.dtype)
    logits = lax.dot_general(rw_ref[...], h, (((1,), (1,)), ((), ())), preferred_element_type=F32,
                             precision=lax.Precision.HIGHEST)
    mx = jnp.max(logits, axis=0, keepdims=True)
    ex = jnp.exp(logits - mx)
    aff_ref[...] = ex / jnp.sum(ex, axis=0, keepdims=True)


def _outproj(merged, x, wo, g2, router_w_t, tm=512):
    m = x.shape[0]
    return pl.pallas_call(
        _outproj_kernel,
        grid=(m // tm,),
        in_specs=[
            pl.BlockSpec((tm, D_MODEL), lambda i: (i, 0)),
            pl.BlockSpec((tm, D_MODEL), lambda i: (i, 0)),
            pl.BlockSpec((D_MODEL, D_MODEL), lambda i: (0, 0)),
            pl.BlockSpec((1, D_MODEL), lambda i: (0, 0)),
            pl.BlockSpec((N_EXPERTS, D_MODEL), lambda i: (0, 0)),
        ],
        out_specs=[
            pl.BlockSpec((tm, D_MODEL), lambda i: (i, 0)),
            pl.BlockSpec((tm, D_MODEL), lambda i: (i, 0)),
            pl.BlockSpec((N_EXPERTS, tm), lambda i: (0, i)),
        ],
        out_shape=[
            jax.ShapeDtypeStruct((m, D_MODEL), F32),
            jax.ShapeDtypeStruct((m, D_MODEL), F32),
            jax.ShapeDtypeStruct((N_EXPERTS, m), F32),
        ],
        compiler_params=_params("parallel"),
        name="outproj",
    )(merged, x, wo, g2.reshape(1, D_MODEL), router_w_t)


ROUTE_CHUNK = 1024


def _lane_cumsum(m, inc_scr, tot_scr, off_scr):
    e_count, nt, lanes = m.shape
    m2 = m.reshape(e_count * nt, lanes).astype(BF16)
    r = lax.broadcasted_iota(jnp.int32, (lanes, lanes), 0)
    c = lax.broadcasted_iota(jnp.int32, (lanes, lanes), 1)
    upper = jnp.where(r <= c, 1.0, 0.0).astype(BF16)
    ones = jnp.ones((lanes, lanes), BF16)
    inc_scr[...] = jnp.dot(m2, upper, preferred_element_type=F32).reshape(e_count, nt, lanes)
    tot_scr[...] = jnp.dot(m2, ones, preferred_element_type=F32).reshape(e_count, nt, lanes)
    rr = lax.broadcasted_iota(jnp.int32, (nt, nt), 0)
    cc = lax.broadcasted_iota(jnp.int32, (nt, nt), 1)
    lower = jnp.where(cc < rr, 1.0, 0.0).astype(BF16)
    for e in range(e_count):
        off_scr[e] = jnp.dot(lower, tot_scr[e].astype(BF16), preferred_element_type=F32)


def _route_kernel(aff_ref, idx_ref, gate_ref, inc_scr, tot_scr, off_scr, *, cap, base):
    e_count, nt, lanes = aff_ref.shape
    e = pl.program_id(0)

    @pl.when(e == 0)
    def _():
        bits = lax.bitcast_convert_type(aff_ref[...], jnp.int32)

        def search(b, cur):
            cand = cur | jnp.left_shift(jnp.int32(1), 30 - b)
            cnt = jnp.sum(jnp.where(bits >= cand, 1.0, 0.0), axis=(1, 2), keepdims=True)
            return jnp.where(cnt >= cap, cand, cur)

        thr = lax.fori_loop(0, 31, search, jnp.zeros((e_count, 1, 1), jnp.int32))
        gt = bits > thr
        eq = bits == thr
        need = cap - jnp.sum(jnp.where(gt, 1.0, 0.0), axis=(1, 2), keepdims=True)
        eqf = jnp.where(eq, 1.0, 0.0)
        _lane_cumsum(eqf, inc_scr, tot_scr, off_scr)
        rank = off_scr[...] + inc_scr[...] - eqf
        sel = jnp.where(gt, 1.0, jnp.where(eq & (rank < need), 1.0, 0.0))
        _lane_cumsum(sel, inc_scr, tot_scr, off_scr)

    inc = inc_scr[e].astype(BF16)
    off = off_scr[e]
    cend_row = jnp.transpose(off + tot_scr[e])[0:1, :]
    off_row = jnp.transpose(off)[0:1, :]
    aff_e = aff_ref[e]
    ch = min(ROUTE_CHUNK, cap)
    row_lane = lax.broadcasted_iota(jnp.int32, (ch, nt), 1).astype(F32)
    lane = lax.broadcasted_iota(jnp.int32, (ch, lanes), 1).astype(F32)

    def chunk(ci, carry):
        s0 = pl.multiple_of(ci * ch, ch)
        slot = (s0 + lax.broadcasted_iota(jnp.int32, (ch, 1), 0)).astype(F32)
        krow = jnp.sum(jnp.where(cend_row <= slot, 1.0, 0.0), axis=1, keepdims=True)
        hit = row_lane == krow
        onehot = jnp.where(hit, 1.0, 0.0)
        loc = jnp.dot(onehot.astype(BF16), inc, preferred_element_type=F32)
        target = slot - jnp.sum(jnp.where(hit, off_row, 0.0), axis=1, keepdims=True)
        l = jnp.sum(jnp.where(loc <= target, 1.0, 0.0), axis=1, keepdims=True)
        aff_row = jnp.dot(onehot, aff_e, preferred_element_type=F32, precision=lax.Precision.HIGHEST)
        gate_ref[pl.ds(s0, ch), :] = jnp.sum(jnp.where(lane == l, aff_row, 0.0), axis=1, keepdims=True)
        idx_ref[pl.ds(s0, ch), :] = (krow * lanes + l).astype(jnp.int32) + base
        return carry

    lax.fori_loop(0, cap // ch, chunk, 0)


def _route(aff_t, cap, base):
    e_count, t = aff_t.shape
    nt = t // LANE
    idx, gate = pl.pallas_call(
        functools.partial(_route_kernel, cap=cap, base=base),
        grid=(e_count,),
        in_specs=[pl.BlockSpec((e_count, nt, LANE), lambda e: (0, 0, 0))],
        out_specs=[pl.BlockSpec((None, cap, 1), lambda e: (e, 0, 0)), pl.BlockSpec((None, cap, 1), lambda e: (e, 0, 0))],
        out_shape=[jax.ShapeDtypeStruct((e_count, cap, 1), jnp.int32), jax.ShapeDtypeStruct((e_count, cap, 1), F32)],
        scratch_shapes=[pltpu.VMEM((e_count, nt, LANE), F32)] * 3,
        compiler_params=_params("arbitrary"),
        name="route",
    )(aff_t.reshape(e_count, nt, LANE))
    return idx.reshape(e_count * cap), gate


def _ffn_kernel(idx_ref, h_hbm, y_hbm, gate_ref, wg_ref, wu_ref, wd_ref, yo_hbm, xf, xb, acc_ref, sem_x, sem_y,
                sem_s, *, tm):
    del y_hbm
    ei, i, f = pl.program_id(0), pl.program_id(1), pl.program_id(2)
    n_i, n_f = pl.num_programs(1), pl.num_programs(2)
    tile = ei * n_i + i
    n_tiles = pl.num_programs(0) * n_i
    cur = tile % 2

    def start_gather(src_hbm, slot, which_tile, sem):
        base = which_tile * tm

        def body(r, carry):
            row = idx_ref[base + r]
            pltpu.make_async_copy(src_hbm.at[pl.ds(row, 1), :], xf.at[slot, pl.ds(r, 1), :], sem).start()
            return carry

        lax.fori_loop(0, tm, body, 0, unroll=8)

    def wait_tile(slot, sem):
        pltpu.make_async_copy(h_hbm.at[pl.ds(0, tm), :], xf.at[slot], sem).wait()

    @pl.when(f == 0)
    def _():
        @pl.when(tile == 0)
        def _():
            start_gather(h_hbm, 0, 0, sem_x.at[0])

        wait_tile(cur, sem_x.at[cur])
        xb[...] = xf[cur].astype(BF16)
        start_gather(yo_hbm, cur, tile, sem_y.at[0])

        @pl.when(tile + 1 < n_tiles)
        def _():
            start_gather(h_hbm, 1 - cur, tile + 1, sem_x.at[1 - cur])

        acc_ref[...] = jnp.zeros_like(acc_ref)

    x = xb[...]
    hg = jnp.dot(x, wg_ref[...], preferred_element_type=F32)
    hu = jnp.dot(x, wu_ref[...], preferred_element_type=F32)
    hid = (jax.nn.silu(hg) * hu).astype(BF16)
    acc_ref[...] += jnp.dot(hid, wd_ref[...], preferred_element_type=F32)

    @pl.when(f == n_f - 1)
    def _():
        wait_tile(cur, sem_y.at[0])
        xf[cur] = xf[cur] + acc_ref[...] * gate_ref[...]
        base = tile * tm

        def body(r, carry):
            row = idx_ref[base + r]
            pltpu.make_async_copy(xf.at[cur, pl.ds(r, 1), :], yo_hbm.at[pl.ds(row, 1), :], sem_s.at[0]).start()
            return carry

        lax.fori_loop(0, tm, body, 0, unroll=8)
        pltpu.make_async_copy(xf.at[cur], yo_hbm.at[pl.ds(0, tm), :], sem_s.at[0]).wait()


def _ffn(idx, h, y, gate, wg, wu, wd, tm=1024, tf=256):
    e, cap, _ = gate.shape
    d = h.shape[1]
    ff = wg.shape[2]
    tm = min(tm, cap)
    grid_spec = pltpu.PrefetchScalarGridSpec(
        num_scalar_prefetch=1,
        grid=(e, cap // tm, ff // tf),
        in_specs=[
            pl.BlockSpec(memory_space=pl.ANY),
            pl.BlockSpec(memory_space=pl.ANY),
            pl.BlockSpec((None, tm, 1), lambda ei, i, f, idx_ref: (ei, i, 0)),
            pl.BlockSpec((None, d, tf), lambda ei, i, f, idx_ref: (ei, 0, f)),
            pl.BlockSpec((None, d, tf), lambda ei, i, f, idx_ref: (ei, 0, f)),
            pl.BlockSpec((None, tf, d), lambda ei, i, f, idx_ref: (ei, f, 0)),
        ],
        out_specs=pl.BlockSpec(memory_space=pl.ANY),
        scratch_shapes=[
            pltpu.VMEM((2, tm, d), F32),
            pltpu.VMEM((tm, d), BF16),
            pltpu.VMEM((tm, d), F32),
            pltpu.SemaphoreType.DMA((2,)),
            pltpu.SemaphoreType.DMA((1,)),
            pltpu.SemaphoreType.DMA((1,)),
        ],
    )
    return pl.pallas_call(
        functools.partial(_ffn_kernel, tm=tm),
        grid_spec=grid_spec,
        out_shape=jax.ShapeDtypeStruct(y.shape, y.dtype),
        input_output_aliases={2: 0},
        compiler_params=_params("arbitrary", "arbitrary", "arbitrary"),
        name="expert_ffn",
    )(idx, h, y, gate, wg, wu, wd)


def _final_norm_kernel(x_ref, g_ref, o_ref):
    x = x_ref[...]
    ms = jnp.mean(x * x, axis=-1, keepdims=True)
    o_ref[...] = x * lax.rsqrt(ms + EPS) * g_ref[...]


def _final_norm(x, g, tm=1024):
    m, d = x.shape
    return pl.pallas_call(
        _final_norm_kernel,
        grid=(m // tm,),
        in_specs=[pl.BlockSpec((tm, d), lambda i: (i, 0)), pl.BlockSpec((1, d), lambda i: (0, 0))],
        out_specs=pl.BlockSpec((tm, d), lambda i: (i, 0)),
        out_shape=jax.ShapeDtypeStruct((m, d), F32),
        compiler_params=_params("parallel"),
        name="final_norm",
    )(x, g.reshape(1, d))


def _split_w_in(w):
    aw = ATTN_OUT_WIDTH
    part = lambda which, gi: w[:, which * ATTN_WIDTH + gi * aw:which * ATTN_WIDTH + (gi + 1) * aw]
    rest = 3 * ATTN_WIDTH
    w_main = jnp.concatenate(
        [w[:, rest:rest + POOL_WIDTH + SSM_WIDTH], part(0, 0), part(1, 0), part(2, 0),
         w[:, rest + POOL_WIDTH + SSM_WIDTH:]], axis=1).astype(BF16)
    w_dil = [jnp.concatenate([part(0, gi), part(1, gi), part(2, gi)], axis=1).astype(BF16)
             for gi in range(1, len(DILATED_GROUPS))]
    return w_main, w_dil


def _mixer(x, nb, norm_g, w_main, w_dil, pool_w, pool_scale, tables, d_skip, glu_w_t, wa, wp, ws, wo, norm2_g,
           router_w_t):
    proj, h = _inproj(x, norm_g, w_main)
    attn = [_attn_group(proj.reshape(nb, SEQ, MAIN_WIDTH), 0, nb, Q_OFF)]
    for gi in range(1, len(DILATED_GROUPS)):
        dil = DILATED_GROUPS[gi][1]
        qkv = _dilproj(h, w_dil[gi - 1], dil)
        attn.append(_attn_group(qkv.reshape(nb, SEQ // dil, dil * QKV_WIDTH), gi, nb, 0))
    attn_o = [a[0].reshape(-1, a[0].shape[2]) for a in attn]
    attn_l = [a[1].reshape(-1, a[1].shape[2]) for a in attn]
    pool_y = _pool(proj, pool_w, pool_scale, nb)

    u = proj[:, SSM_OFF:SSM_OFF + SSM_WIDTH].reshape(nb, N_CHUNKS, SSM_CHUNK, SSM_WIDTH)
    u_t = jnp.transpose(u, (3, 1, 0, 2)).reshape(SSM_WIDTH, N_CHUNKS * nb, SSM_CHUNK)
    y_t = _s5_mix(u_t, *tables, nb)
    ssm_y = _s5_glu(y_t.reshape(SSM_WIDTH, -1), u_t.reshape(SSM_WIDTH, -1), d_skip, glu_w_t, nb)

    merged = _merge(attn_o, attn_l, pool_y, ssm_y, proj, wa, wp, ws)
    return _outproj(merged, x, wo, norm2_g, router_w_t)


def _experts(y, h, aff_t, t0, t, wg, wu, wd):
    cap = CAPACITY_FACTOR * t // N_EXPERTS
    idx, gate = _route(aff_t[:, t0:t0 + t], cap, t0)
    return _ffn(idx, h, y, gate, wg, wu, wd)


def kernel(x_prompt, x_sample, norm1_g, w_in, pool_w, pool_scale, ssm_a_re, ssm_a_im, ssm_log_step, ssm_b_re,
           ssm_b_im, ssm_c_re, ssm_c_im, ssm_d, ssm_glu_w, w_br_attn, w_br_pool, w_br_ssm, w_out, norm2_g,
           router_w, w_gate, w_up, w_down, final_norm_g):
    nbp, nbs = x_prompt.shape[0], x_sample.shape[0]
    nb = nbp + nbs
    tp, ts = nbp * SEQ, nbs * SEQ
    x = jnp.concatenate([x_prompt, x_sample], axis=0).reshape(nb * SEQ, D_MODEL)
    for l in range(DEPTH):
        w_main, w_dil = _split_w_in(w_in[l])
        tables = _s5_tables(ssm_a_re[l], ssm_a_im[l], ssm_log_step[l], ssm_b_re[l], ssm_b_im[l], ssm_c_re[l],
                            ssm_c_im[l])
        x, h, aff_t = _mixer(
            x, nb, norm1_g[l], w_main, w_dil, pool_w[l], pool_scale[l], tables, ssm_d[l],
            ssm_glu_w[l].T.astype(BF16), w_br_attn[l].astype(BF16), w_br_pool[l].astype(BF16),
            w_br_ssm[l].astype(BF16), w_out[l].astype(BF16), norm2_g[l], router_w[l].T)
        wg, wu, wd = w_gate[l].astype(BF16), w_up[l].astype(BF16), w_down[l].astype(BF16)
        x = _experts(x, h, aff_t, 0, tp, wg, wu, wd)
        x = _experts(x, h, aff_t, tp, ts, wg, wu, wd)
    y = _final_norm(x, final_norm_g).reshape(nb, SEQ, D_MODEL)
    return (y[:nbp], y[nbp:])
```

```python
import functools
import math

import jax
import jax.numpy as jnp
from jax import lax
from jax.experimental import pallas as pl
from jax.experimental.pallas import tpu as pltpu

F32 = jnp.float32
BF16 = jnp.bfloat16

D_MODEL = 2048
SEQ = 2048
DEPTH = 4
EPS = 1e-6
HEAD_DIM = 128
HEADS_PER_GROUP = 4
DILATED_GROUPS = ((128, 1), (512, 4), (2048, 16))
N_ATTN_HEADS = HEADS_PER_GROUP * len(DILATED_GROUPS)
ATTN_WIDTH = N_ATTN_HEADS * HEAD_DIM
ATTN_OUT_WIDTH = HEADS_PER_GROUP * HEAD_DIM
ALIBI_SLOPES = tuple(2.0 ** (-8.0 * (h + 1) / N_ATTN_HEADS) for h in range(N_ATTN_HEADS))
POOL_WINDOWS = (2, 4, 8, 16)
POOL_WIDTH = D_MODEL // 2
POOL_GROUP = POOL_WIDTH // len(POOL_WINDOWS)
SSM_WIDTH = D_MODEL // 2
SSM_GROUP = 16
SSM_GROUPS = SSM_WIDTH // SSM_GROUP
SSM_STATE = 64
IN_WIDTH = 3 * ATTN_WIDTH + POOL_WIDTH + SSM_WIDTH + 3 * D_MODEL
N_EXPERTS = 16
EXPERT_FF = 11 * D_MODEL // 8
CAPACITY_FACTOR = 2

POOL_OFF = 0
SSM_OFF = POOL_OFF + POOL_WIDTH
Q_OFF = SSM_OFF + SSM_WIDTH
K_OFF = Q_OFF + ATTN_OUT_WIDTH
V_OFF = K_OFF + ATTN_OUT_WIDTH
GATE_OFF = V_OFF + ATTN_OUT_WIDTH
MAIN_WIDTH = GATE_OFF + 3 * D_MODEL
QKV_WIDTH = 3 * ATTN_OUT_WIDTH

LANE = 128
SSM_CHUNK = LANE
N_CHUNKS = SEQ // SSM_CHUNK
ATTN_QB = 128
ATTN_HALF = 64
VMEM_LIMIT = 48 * 1024 * 1024
FFN_VMEM_LIMIT = 56 * 1024 * 1024


def _params(*sem):
    return pltpu.CompilerParams(dimension_semantics=sem, vmem_limit_bytes=VMEM_LIMIT)


def _inproj_kernel(x_ref, g_ref, w_ref, o_ref, h_ref):
    @pl.when(pl.program_id(1) == 0)
    def _():
        x = x_ref[...]
        ms = jnp.mean(x * x, axis=-1, keepdims=True)
        h_ref[...] = (x * lax.rsqrt(ms + EPS) * g_ref[...]).astype(BF16)

    o_ref[...] = jnp.dot(h_ref[...], w_ref[...], preferred_element_type=F32).astype(o_ref.dtype)


def _inproj(x, g, w, tm=1024, tn=512):
    m, d = x.shape
    n = w.shape[1]
    return pl.pallas_call(
        _inproj_kernel,
        grid=(m // tm, n // tn),
        in_specs=[
            pl.BlockSpec((tm, d), lambda i, j: (i, 0)),
            pl.BlockSpec((1, d), lambda i, j: (0, 0)),
            pl.BlockSpec((d, tn), lambda i, j: (0, j)),
        ],
        out_specs=[pl.BlockSpec((tm, tn), lambda i, j: (i, j)), pl.BlockSpec((tm, d), lambda i, j: (i, 0))],
        out_shape=[jax.ShapeDtypeStruct((m, n), BF16), jax.ShapeDtypeStruct((m, d), BF16)],
        compiler_params=_params("parallel", "arbitrary"),
        name="inproj",
    )(x, g.reshape(1, d), w)


def _dilproj_kernel(h_ref, w_ref, o_ref, y_scr, *, dil):
    rows = o_ref.shape[0]
    y = jnp.dot(h_ref[...], w_ref[...], preferred_element_type=F32)
    for c in range(QKV_WIDTH // LANE):
        y_scr[c] = y[:, c * LANE:(c + 1) * LANE]
    for r in range(dil):
        for c in range(QKV_WIDTH // LANE):
            col = r * QKV_WIDTH + c * LANE
            o_ref[:, col:col + LANE] = y_scr[c, pl.ds(r, rows, stride=dil), :].astype(o_ref.dtype)


def _dilproj(h, w, dil, tm=512):
    m, d = h.shape
    return pl.pallas_call(
        functools.partial(_dilproj_kernel, dil=dil),
        grid=(m // tm,),
        in_specs=[pl.BlockSpec((tm, d), lambda i: (i, 0)), pl.BlockSpec((d, QKV_WIDTH), lambda i: (0, 0))],
        out_specs=pl.BlockSpec((tm // dil, dil * QKV_WIDTH), lambda i: (i, 0)),
        out_shape=jax.ShapeDtypeStruct((m // dil, dil * QKV_WIDTH), BF16),
        scratch_shapes=[pltpu.VMEM((QKV_WIDTH // LANE, tm, LANE), F32)],
        compiler_params=_params("parallel"),
        name=f"dilproj_d{dil}",
    )(h, w)


def _attn_kernel(slope_ref, q_ref, k_ref, v_ref, o_ref, l_ref, *, n, dil):
    kw = min(2 * ATTN_QB, n)
    nqb = n // ATTN_QB
    scale = HEAD_DIM ** -0.5

    def body(j, carry):
        qs = pl.multiple_of(j * ATTN_QB, ATTN_QB)
        ks = pl.multiple_of(jnp.clip(qs - ATTN_HALF, 0, n - kw), ATTN_HALF)
        qpos = qs + lax.broadcasted_iota(jnp.int32, (ATTN_QB, kw), 0)
        kpos = ks + lax.broadcasted_iota(jnp.int32, (ATTN_QB, kw), 1)
        rel = jnp.abs(kpos - qpos)
        valid = rel <= ATTN_HALF
        relf = rel.astype(F32)
        for h in range(HEADS_PER_GROUP):
            cols = slice(h * HEAD_DIM, (h + 1) * HEAD_DIM)
            q = q_ref[pl.ds(qs, ATTN_QB), cols]
            k = k_ref[pl.ds(ks, kw), cols]
            v = v_ref[pl.ds(ks, kw), cols]
            s = lax.dot_general(q, k, (((1,), (1,)), ((), ())), preferred_element_type=F32) * scale
            neg_slope = -slope_ref[h][:, :kw] * float(dil)
            s = jnp.where(valid, s + neg_slope * relf, -jnp.inf)
            m = jnp.max(s, axis=1, keepdims=True)
            p = jnp.exp(s - m)
            l = jnp.sum(p, axis=1, keepdims=True)
            o = jnp.dot(p.astype(BF16), v, preferred_element_type=F32) / l
            o_ref[pl.ds(qs, ATTN_QB), cols] = o.astype(o_ref.dtype)
            l_ref[pl.ds(qs, ATTN_QB), cols] = jnp.broadcast_to(m + jnp.log(l), (ATTN_QB, HEAD_DIM))
        return carry

    lax.fori_loop(0, nqb, body, 0)


def _attn_group(qkv, gi, nb, col0):
    _, dil = DILATED_GROUPS[gi]
    n = SEQ // dil
    aw = ATTN_OUT_WIDTH
    cb = qkv.shape[2] // dil // aw
    slopes = jnp.asarray(ALIBI_SLOPES[gi * HEADS_PER_GROUP:(gi + 1) * HEADS_PER_GROUP], F32)
    slopes = jnp.broadcast_to(slopes[:, None, None], (HEADS_PER_GROUP, 1, 2 * ATTN_QB))

    def col(which):
        return lambda b, r: (b, 0, r * cb + col0 // aw + which)

    out_map = lambda b, r: (b, 0, r)
    return pl.pallas_call(
        functools.partial(_attn_kernel, n=n, dil=dil),
        grid=(nb, dil),
        in_specs=[
            pl.BlockSpec((HEADS_PER_GROUP, 1, 2 * ATTN_QB), lambda b, r: (0, 0, 0)),
            pl.BlockSpec((None, n, aw), col(0)),
            pl.BlockSpec((None, n, aw), col(1)),
            pl.BlockSpec((None, n, aw), col(2)),
        ],
        out_specs=[pl.BlockSpec((None, n, aw), out_map), pl.BlockSpec((None, n, aw), out_map)],
        out_shape=[
            jax.ShapeDtypeStruct((nb, n, dil * aw), BF16),
            jax.ShapeDtypeStruct((nb, n, dil * aw), F32),
        ],
        compiler_params=_params("parallel", "parallel"),
        name=f"attn_g{gi}",
    )(slopes, qkv, qkv, qkv)


def _shift_rows(x, k, row):
    n = x.shape[0]
    if k > 0:
        return jnp.where(row >= k, pltpu.roll(x, k, 0), 0.0)
    return jnp.where(row < n + k, pltpu.roll(x, n + k, 0), 0.0)


def _pool_kernel(u_ref, w_ref, s_ref, o_ref):
    n = u_ref.shape[0]
    row = lax.broadcasted_iota(jnp.int32, (n, POOL_GROUP), 0)
    for gi, win in enumerate(POOL_WINDOWS):
        half = win // 2
        cols = slice(gi * POOL_GROUP, (gi + 1) * POOL_GROUP)
        u = u_ref[:, cols].astype(F32)
        back = _shift_rows(u, 1, row)
        fwd = u
        step = 1
        while step < half:
            back = back + _shift_rows(back, step, row)
            fwd = fwd + _shift_rows(fwd, -step, row)
            step *= 2
        t = row[:, :1]
        lo = jnp.clip(t - half, 0, n - 1)
        hi = jnp.clip(t + half - 1, 0, n - 1)
        cnt = (hi - lo + 1).astype(F32)
        pooled = (back + fwd) / cnt - u
        y = jnp.dot(pooled.astype(BF16), w_ref[gi], preferred_element_type=F32)
        o_ref[:, cols] = (y * s_ref[:, cols]).astype(o_ref.dtype)


def _pool(proj, pool_w, pool_scale, nb):
    p3 = proj.reshape(nb, SEQ, MAIN_WIDTH)
    out = pl.pallas_call(
        _pool_kernel,
        grid=(nb,),
        in_specs=[
            pl.BlockSpec((None, SEQ, POOL_WIDTH), lambda b: (b, 0, POOL_OFF // POOL_WIDTH)),
            pl.BlockSpec((len(POOL_WINDOWS), POOL_GROUP, POOL_GROUP), lambda b: (0, 0, 0)),
            pl.BlockSpec((1, POOL_WIDTH), lambda b: (0, 0)),
        ],
        out_specs=pl.BlockSpec((None, SEQ, POOL_WIDTH), lambda b: (b, 0, 0)),
        out_shape=jax.ShapeDtypeStruct((nb, SEQ, POOL_WIDTH), BF16),
        compiler_params=_params("parallel"),
        name="pool",
    )(p3, pool_w.astype(BF16), pool_scale.reshape(1, POOL_WIDTH))
    return out.reshape(nb * SEQ, POOL_WIDTH)


def _s5_tables(a_re, a_im, log_step, b_re, b_im, c_re, c_im):
    t_len = SSM_CHUNK
    hp = lax.Precision.HIGHEST
    lr, li = a_re.astype(F32), a_im.astype(F32)
    dt = jnp.exp(log_step.astype(F32))[..., None]
    mag = jnp.exp(lr * dt)
    ab_r, ab_i = mag * jnp.cos(li * dt), mag * jnp.sin(li * dt)
    den = lr * lr + li * li
    xr, xi = ab_r - 1.0, ab_i
    co_r = (xr * lr + xi * li) / den
    co_i = (xi * lr - xr * li) / den
    br, bi = b_re.astype(F32), b_im.astype(F32)
    bb_r = co_r[..., None] * br - co_i[..., None] * bi
    bb_i = co_r[..., None] * bi + co_i[..., None] * br
    cr, ci = c_re.astype(F32), c_im.astype(F32)

    tau = jnp.arange(t_len + 1, dtype=F32)
    pmag = jnp.exp((lr * dt)[..., None] * tau)
    pang = (li * dt)[..., None] * tau
    pw_r, pw_i = pmag * jnp.cos(pang), pmag * jnp.sin(pang)

    cb_r = cr[:, :, :, None, :] * jnp.swapaxes(bb_r, 2, 3)[:, :, None, :, :] \
        - ci[:, :, :, None, :] * jnp.swapaxes(bb_i, 2, 3)[:, :, None, :, :]
    cb_i = cr[:, :, :, None, :] * jnp.swapaxes(bb_i, 2, 3)[:, :, None, :, :] \
        + ci[:, :, :, None, :] * jnp.swapaxes(bb_r, 2, 3)[:, :, None, :, :]
    kern = jnp.einsum('dgabp,dgpt->dgabt', cb_r, pw_r[..., :t_len], precision=hp) \
        - jnp.einsum('dgabp,dgpt->dgabt', cb_i, pw_i[..., :t_len], precision=hp)
    kf, kb = kern[0], kern[1]
    zero = jnp.zeros_like(kf[..., :1])
    kvec = jnp.concatenate([zero, kb[..., :0:-1], kf[..., :1] + kb[..., :1], kf[..., 1:]], axis=-1)
    kvec = jnp.swapaxes(kvec, 1, 2).reshape(SSM_GROUPS, SSM_GROUP * SSM_GROUP, 2 * t_len)

    def cmul(ar, ai, xr_, xi_):
        return ar * xr_ - ai * xi_, ar * xi_ + ai * xr_

    pf_r, pf_i = pw_r[0][..., t_len - 1::-1][..., :t_len], pw_i[0][..., t_len - 1::-1][..., :t_len]
    pb_r, pb_i = pw_r[1][..., :t_len], pw_i[1][..., :t_len]
    ef_r, ef_i = cmul(pf_r[:, :, None, :], pf_i[:, :, None, :], bb_r[0][..., None], bb_i[0][..., None])
    eb_r, eb_i = cmul(pb_r[:, :, None, :], pb_i[:, :, None, :], bb_r[1][..., None], bb_i[1][..., None])
    to_e = lambda a: jnp.transpose(a, (0, 2, 3, 1)).reshape(SSM_GROUPS, SSM_GROUP * t_len, SSM_STATE)
    e_tab = jnp.concatenate([to_e(ef_r), to_e(ef_i), to_e(eb_r), to_e(eb_i)], axis=-1)

    qf_r, qf_i = pw_r[0][..., 1:], pw_i[0][..., 1:]
    qb_r, qb_i = pw_r[1][..., :0:-1], pw_i[1][..., :0:-1]
    wf_r, wf_i = cmul(jnp.swapaxes(cr[0], 1, 2)[..., None], jnp.swapaxes(ci[0], 1, 2)[..., None],
                      qf_r[:, :, None, :], qf_i[:, :, None, :])
    wb_r, wb_i = cmul(jnp.swapaxes(cr[1], 1, 2)[..., None], jnp.swapaxes(ci[1], 1, 2)[..., None],
                      qb_r[:, :, None, :], qb_i[:, :, None, :])
    to_f = lambda a: a.reshape(SSM_GROUPS, SSM_STATE, SSM_GROUP * t_len)
    f_tab = jnp.concatenate([to_f(wf_r), to_f(-wf_i), to_f(wb_r), to_f(-wb_i)], axis=1)

    at_r, at_i = pw_r[..., t_len], pw_i[..., t_len]
    row0 = jnp.concatenate([at_r[0], at_r[0], at_r[1], at_r[1]], axis=-1)
    row1 = jnp.concatenate([-at_i[0], at_i[0], -at_i[1], at_i[1]], axis=-1)
    dec = jnp.concatenate([row0[:, None, :], row1[:, None, :], jnp.zeros((SSM_GROUPS, 6, 4 * SSM_STATE), F32)], axis=1)
    return kvec, e_tab.astype(BF16), f_tab.astype(BF16), dec


def _s5_kernel(u_ref, kvec_ref, e_ref, f_ref, dec_ref, y_ref, m_scr, hc_scr, *, nb):
    t_len = SSM_CHUNK
    h = SSM_GROUP

    def build(hi, carry):
        for ho in range(h):
            row = kvec_ref[pl.ds(hi * h + ho, 1), :]
            rolled = pltpu.roll(jnp.broadcast_to(row, (t_len, 2 * t_len)), 0, 1, stride=1, stride_axis=0)
            m_scr[pl.ds(pl.multiple_of(hi * t_len, t_len), t_len), ho * t_len:(ho + 1) * t_len] = (
                rolled[:, t_len:].astype(BF16))
        return carry

    lax.fori_loop(0, h, build, 0)

    x = jnp.concatenate([u_ref[hi] for hi in range(h)], axis=1)
    s = jnp.dot(x, e_ref[...], preferred_element_type=F32)

    d0, d1 = dec_ref[0:1, :], dec_ref[1:2, :]
    for half, order in ((0, range(N_CHUNKS)), (1, range(N_CHUNKS - 1, -1, -1))):
        lanes = slice(half * 2 * SSM_STATE, (half + 1) * 2 * SSM_STATE)
        a0, a1 = d0[:, lanes], d1[:, lanes]
        state = jnp.zeros((nb, 2 * SSM_STATE), F32)
        for c in order:
            rows = slice(c * nb, (c + 1) * nb)
            hc_scr[rows, lanes] = state
            state = a0 * state + a1 * pltpu.roll(state, SSM_STATE, 1) + s[rows, lanes]

    hc = hc_scr[...]
    hc_hi = hc.astype(BF16)
    hc_lo = (hc - hc_hi.astype(F32)).astype(BF16)
    for ho in range(0, h, 2):
        cols = slice(ho * t_len, (ho + 2) * t_len)
        y = jnp.dot(x, m_scr[:, cols], preferred_element_type=F32)
        y = y + jnp.dot(hc_hi, f_ref[:, cols], preferred_element_type=F32)
        y = y + jnp.dot(hc_lo, f_ref[:, cols], preferred_element_type=F32)
        y_ref[ho] = y[:, :t_len]
        y_ref[ho + 1] = y[:, t_len:]


def _s5_mix(u_t, kvec, e_tab, f_tab, dec, nb):
    rows = N_CHUNKS * nb
    tl = SSM_CHUNK
    return pl.pallas_call(
        functools.partial(_s5_kernel, nb=nb),
        grid=(SSM_GROUPS,),
        in_specs=[
            pl.BlockSpec((SSM_GROUP, rows, tl), lambda g: (g, 0, 0)),
            pl.BlockSpec((None, SSM_GROUP * SSM_GROUP, 2 * tl), lambda g: (g, 0, 0)),
            pl.BlockSpec((None, SSM_GROUP * tl, 4 * SSM_STATE), lambda g: (g, 0, 0)),
            pl.BlockSpec((None, 4 * SSM_STATE, SSM_GROUP * tl), lambda g: (g, 0, 0)),
            pl.BlockSpec((None, 8, 4 * SSM_STATE), lambda g: (g, 0, 0)),
        ],
        out_specs=pl.BlockSpec((SSM_GROUP, rows, tl), lambda g: (g, 0, 0)),
        out_shape=jax.ShapeDtypeStruct((SSM_WIDTH, rows, tl), F32),
        scratch_shapes=[
            pltpu.VMEM((SSM_GROUP * tl, SSM_GROUP * tl), BF16),
            pltpu.VMEM((rows, 4 * SSM_STATE), F32),
        ],
        compiler_params=_params("parallel"),
        name="s5_mix",
    )(u_t, kvec, e_tab, f_tab, dec)


def _glu_kernel(y_ref, u_ref, d_ref, w_ref, o_ref):
    y = y_ref[...] + d_ref[...] * u_ref[...].astype(F32)
    g = jax.nn.gelu(y)
    z = jnp.dot(w_ref[...], g.astype(BF16), preferred_element_type=F32)
    o_ref[...] = (g * jax.nn.sigmoid(z)).T.astype(o_ref.dtype)


def _s5_glu(y_t, u_t, d_skip, glu_w_t, nb):
    tokens = y_t.shape[1]
    tl = SSM_CHUNK
    return pl.pallas_call(
        _glu_kernel,
        grid=(tokens // tl,),
        in_specs=[
            pl.BlockSpec((SSM_WIDTH, tl), lambda i: (0, i)),
            pl.BlockSpec((SSM_WIDTH, tl), lambda i: (0, i)),
            pl.BlockSpec((SSM_WIDTH, 1), lambda i: (0, 0)),
            pl.BlockSpec((SSM_WIDTH, SSM_WIDTH), lambda i: (0, 0)),
        ],
        out_specs=pl.BlockSpec((tl, SSM_WIDTH), lambda i: ((i % nb) * N_CHUNKS + i // nb, 0)),
        out_shape=jax.ShapeDtypeStruct((tokens, SSM_WIDTH), BF16),
        compiler_params=_params("parallel"),
        name="s5_glu",
    )(y_t, u_t, d_skip.reshape(SSM_WIDTH, 1), glu_w_t)


def _merge_kernel(o1, o2, o3, l1, l2, l3, pool_ref, ssm_ref, ga, gp, gs, wa, wp, ws, out_ref, attn_scr, os_scr, ls_scr):
    @pl.when(pl.program_id(1) == 0)
    def _():
        aw = ATTN_OUT_WIDTH
        for gi, (o_ref, l_ref) in enumerate(((o2, l2), (o3, l3))):
            dil = DILATED_GROUPS[gi + 1][1]
            rows = o_ref.shape[0]
            for r in range(dil):
                for c in range(HEADS_PER_GROUP):
                    col = r * aw + c * HEAD_DIM
                    os_scr[gi, c, pl.ds(r, rows, stride=dil), :] = o_ref[:, col:col + HEAD_DIM].astype(F32)
                    ls_scr[gi, c, pl.ds(r, rows, stride=dil), :] = l_ref[:, col:col + HEAD_DIM]
        for c in range(HEADS_PER_GROUP):
            cols = slice(c * HEAD_DIM, (c + 1) * HEAD_DIM)
            a1, a2, a3 = l1[:, cols], ls_scr[0, c], ls_scr[1, c]
            m = jnp.maximum(jnp.maximum(a1, a2), a3)
            e1, e2, e3 = jnp.exp(a1 - m), jnp.exp(a2 - m), jnp.exp(a3 - m)
            inv = 1.0 / (e1 + e2 + e3)
            y = o1[:, cols].astype(F32) * (e1 * inv) + os_scr[0, c] * (e2 * inv) + os_scr[1, c] * (e3 * inv)
            attn_scr[:, cols] = y.astype(BF16)

    ya = jnp.dot(attn_scr[...], wa[...], preferred_element_type=F32)
    yp = jnp.dot(pool_ref[...], wp[...], preferred_element_type=F32)
    ys = jnp.dot(ssm_ref[...], ws[...], preferred_element_type=F32)
    merged = (jax.nn.sigmoid(ga[...].astype(F32)) * ya + jax.nn.sigmoid(gp[...].astype(F32)) * yp
              + jax.nn.sigmoid(gs[...].astype(F32)) * ys)
    out_ref[...] = merged.astype(out_ref.dtype)


def _merge(os_, ls_, pool_y, ssm_y, proj, wa, wp, ws, tm=512, tn=512):
    m = proj.shape[0]
    gb = GATE_OFF // tn
    nj = D_MODEL // tn
    aw = ATTN_OUT_WIDTH
    row = lambda i, j: (i, 0)
    dils = [d for _, d in DILATED_GROUPS]
    attn_specs = [pl.BlockSpec((tm // d, d * aw), row) for d in dils]
    return pl.pallas_call(
        _merge_kernel,
        grid=(m // tm, nj),
        in_specs=attn_specs + attn_specs + [
            pl.BlockSpec((tm, POOL_WIDTH), row),
            pl.BlockSpec((tm, SSM_WIDTH), row),
            pl.BlockSpec((tm, tn), lambda i, j: (i, gb + j)),
            pl.BlockSpec((tm, tn), lambda i, j: (i, gb + nj + j)),
            pl.BlockSpec((tm, tn), lambda i, j: (i, gb + 2 * nj + j)),
            pl.BlockSpec((ATTN_OUT_WIDTH, tn), lambda i, j: (0, j)),
            pl.BlockSpec((POOL_WIDTH, tn), lambda i, j: (0, j)),
            pl.BlockSpec((SSM_WIDTH, tn), lambda i, j: (0, j)),
        ],
        out_specs=pl.BlockSpec((tm, tn), lambda i, j: (i, j)),
        out_shape=jax.ShapeDtypeStruct((m, D_MODEL), BF16),
        scratch_shapes=[
            pltpu.VMEM((tm, aw), BF16),
            pltpu.VMEM((len(dils) - 1, HEADS_PER_GROUP, tm, HEAD_DIM), F32),
            pltpu.VMEM((len(dils) - 1, HEADS_PER_GROUP, tm, HEAD_DIM), F32),
        ],
        compiler_params=_params("parallel", "arbitrary"),
        name="merge",
    )(*os_, *ls_, pool_y, ssm_y, proj, proj, proj, wa, wp, ws)


def _outproj_kernel(mg_ref, x_ref, wo_ref, g_ref, rw_ref, xo_ref, h_ref, aff_ref):
    xn = x_ref[...] + jnp.dot(mg_ref[...], wo_ref[...], preferred_element_type=F32)
    xo_ref[...] = xn
    ms = jnp.mean(xn * xn, axis=-1, keepdims=True)
    h = xn * lax.rsqrt(ms + EPS) * g_ref[...]
    h_ref[...] = h.astype(h_ref.dtype)
    logits = lax.dot_general(rw_ref[...], h, (((1,), (1,)), ((), ())), preferred_element_type=F32,
                             precision=lax.Precision.HIGHEST)
    mx = jnp.max(logits, axis=0, keepdims=True)
    ex = jnp.exp(logits - mx)
    aff_ref[...] = ex / jnp.sum(ex, axis=0, keepdims=True)


def _outproj(merged, x, wo, g2, router_w_t, tm=512):
    m = x.shape[0]
    return pl.pallas_call(
        _outproj_kernel,
        grid=(m // tm,),
        in_specs=[
            pl.BlockSpec((tm, D_MODEL), lambda i: (i, 0)),
            pl.BlockSpec((tm, D_MODEL), lambda i: (i, 0)),
            pl.BlockSpec((D_MODEL, D_MODEL), lambda i: (0, 0)),
            pl.BlockSpec((1, D_MODEL), lambda i: (0, 0)),
            pl.BlockSpec((N_EXPERTS, D_MODEL), lambda i: (0, 0)),
        ],
        out_specs=[
            pl.BlockSpec((tm, D_MODEL), lambda i: (i, 0)),
            pl.BlockSpec((tm, D_MODEL), lambda i: (i, 0)),
            pl.BlockSpec((N_EXPERTS, tm), lambda i: (0, i)),
        ],
        out_shape=[
            jax.ShapeDtypeStruct((m, D_MODEL), F32),
            jax.ShapeDtypeStruct((m, D_MODEL), F32),
            jax.ShapeDtypeStruct((N_EXPERTS, m), F32),
        ],
        compiler_params=_params("parallel"),
        name="outproj",
    )(merged, x, wo, g2.reshape(1, D_MODEL), router_w_t)


ROUTE_CHUNK = 1024


def _lane_cumsum(m, inc_scr, tot_scr, off_scr):
    e_count, nt, lanes = m.shape
    m2 = m.reshape(e_count * nt, lanes).astype(BF16)
    r = lax.broadcasted_iota(jnp.int32, (lanes, lanes), 0)
    c = lax.broadcasted_iota(jnp.int32, (lanes, lanes), 1)
    upper = jnp.where(r <= c, 1.0, 0.0).astype(BF16)
    ones = jnp.ones((lanes, lanes), BF16)
    inc_scr[...] = jnp.dot(m2, upper, preferred_element_type=F32).reshape(e_count, nt, lanes)
    tot_scr[...] = jnp.dot(m2, ones, preferred_element_type=F32).reshape(e_count, nt, lanes)
    rr = lax.broadcasted_iota(jnp.int32, (nt, nt), 0)
    cc = lax.broadcasted_iota(jnp.int32, (nt, nt), 1)
    lower = jnp.where(cc < rr, 1.0, 0.0).astype(BF16)
    for e in range(e_count):
        off_scr[e] = jnp.dot(lower, tot_scr[e].astype(BF16), preferred_element_type=F32)


def _route_kernel(aff_ref, idx_ref, gate_ref, inc_scr, tot_scr, off_scr, *, cap, base):
    e_count, nt, lanes = aff_ref.shape
    e = pl.program_id(0)

    @pl.when(e == 0)
    def _():
        bits = lax.bitcast_convert_type(aff_ref[...], jnp.int32)

        def search(b, cur):
            cand = cur | jnp.left_shift(jnp.int32(1), 30 - b)
            cnt = jnp.sum(jnp.where(bits >= cand, 1.0, 0.0), axis=(1, 2), keepdims=True)
            return jnp.where(cnt >= cap, cand, cur)

        thr = lax.fori_loop(0, 31, search, jnp.zeros((e_count, 1, 1), jnp.int32))
        gt = bits > thr
        eq = bits == thr
        need = cap - jnp.sum(jnp.where(gt, 1.0, 0.0), axis=(1, 2), keepdims=True)
        eqf = jnp.where(eq, 1.0, 0.0)
        _lane_cumsum(eqf, inc_scr, tot_scr, off_scr)
        rank = off_scr[...] + inc_scr[...] - eqf
        sel = jnp.where(gt, 1.0, jnp.where(eq & (rank < need), 1.0, 0.0))
        _lane_cumsum(sel, inc_scr, tot_scr, off_scr)

    inc = inc_scr[e].astype(BF16)
    off = off_scr[e]
    cend_row = jnp.transpose(off + tot_scr[e])[0:1, :]
    off_row = jnp.transpose(off)[0:1, :]
    aff_e = aff_ref[e]
    ch = min(ROUTE_CHUNK, cap)
    row_lane = lax.broadcasted_iota(jnp.int32, (ch, nt), 1).astype(F32)
    lane = lax.broadcasted_iota(jnp.int32, (ch, lanes), 1).astype(F32)

    def chunk(ci, carry):
        s0 = pl.multiple_of(ci * ch, ch)
        slot = (s0 + lax.broadcasted_iota(jnp.int32, (ch, 1), 0)).astype(F32)
        krow = jnp.sum(jnp.where(cend_row <= slot, 1.0, 0.0), axis=1, keepdims=True)
        hit = row_lane == krow
        onehot = jnp.where(hit, 1.0, 0.0)
        loc = jnp.dot(onehot.astype(BF16), inc, preferred_element_type=F32)
        target = slot - jnp.sum(jnp.where(hit, off_row, 0.0), axis=1, keepdims=True)
        l = jnp.sum(jnp.where(loc <= target, 1.0, 0.0), axis=1, keepdims=True)
        aff_row = jnp.dot(onehot, aff_e, preferred_element_type=F32, precision=lax.Precision.HIGHEST)
        gate_ref[pl.ds(s0, ch), :] = jnp.sum(jnp.where(lane == l, aff_row, 0.0), axis=1, keepdims=True)
        idx_ref[pl.ds(s0, ch), :] = (krow * lanes + l).astype(jnp.int32) + base
        return carry

    lax.fori_loop(0, cap // ch, chunk, 0)


def _route(aff_t, cap, base):
    e_count, t = aff_t.shape
    nt = t // LANE
    idx, gate = pl.pallas_call(
        functools.partial(_route_kernel, cap=cap, base=base),
        grid=(e_count,),
        in_specs=[pl.BlockSpec((e_count, nt, LANE), lambda e: (0, 0, 0))],
        out_specs=[pl.BlockSpec((None, cap, 1), lambda e: (e, 0, 0)), pl.BlockSpec((None, cap, 1), lambda e: (e, 0, 0))],
        out_shape=[jax.ShapeDtypeStruct((e_count, cap, 1), jnp.int32), jax.ShapeDtypeStruct((e_count, cap, 1), F32)],
        scratch_shapes=[pltpu.VMEM((e_count, nt, LANE), F32)] * 3,
        compiler_params=_params("arbitrary"),
        name="route",
    )(aff_t.reshape(e_count, nt, LANE))
    return idx.reshape(e_count * cap), gate


def _ffn_kernel(idx_ref, tstart_ref, texp_ref, h_hbm, y_hbm, gate_ref, wg_ref, wu_ref, wd_ref, yo_hbm, bx, by, xb,
                acc_ref, sem_x, sem_y, sem_s, *, tm, n_f):
    del y_hbm, texp_ref
    t, f = pl.program_id(0), pl.program_id(1)
    n_t = pl.num_programs(0)
    cur = t % 2
    prv = 1 - cur
    per_step = tm // n_f
    start_cur = tstart_ref[t]
    start_nxt = tstart_ref[jnp.minimum(t + 1, n_t - 1)]
    start_prv = tstart_ref[jnp.where(t == 0, n_t - 1, t - 1)]

    def x_copy(r, start):
        row = idx_ref[start + r]
        return pltpu.make_async_copy(h_hbm.at[pl.ds(row, 1), :], bx.at[pl.ds(r, 1), :], sem_x.at[0])

    def y_copy(slot, r, start):
        row = idx_ref[start + r]
        return pltpu.make_async_copy(yo_hbm.at[pl.ds(row, 1), :], by.at[slot, pl.ds(r, 1), :], sem_y.at[0])

    def s_copy(slot, r, start):
        row = idx_ref[start + r]
        return pltpu.make_async_copy(by.at[slot, pl.ds(r, 1), :], yo_hbm.at[pl.ds(row, 1), :], sem_s.at[slot])

    def for_rows(lo, hi, fn):
        def body(r, carry):
            fn(r)
            return carry

        lax.fori_loop(lo, hi, body, 0)

    def wait_rows(dst, sem):
        pltpu.make_async_copy(h_hbm.at[pl.ds(0, tm), :], dst, sem).wait()

    @pl.when(f == 0)
    def _():
        @pl.when(t == 0)
        def _():
            for_rows(0, tm, lambda r: x_copy(r, start_cur).start())
            for_rows(0, tm, lambda r: y_copy(prv, r, start_prv).start())
            wait_rows(by.at[prv], sem_y.at[0])

        wait_rows(bx, sem_x.at[0])
        xb[...] = bx[...].astype(BF16)

        @pl.when(t >= 1)
        def _():
            wait_rows(by.at[cur], sem_s.at[cur])

        for_rows(per_step * n_f, tm, lambda r: x_copy(r, start_nxt).start())
        for_rows(per_step * n_f, tm, lambda r: y_copy(cur, r, start_cur).start())
        for_rows(per_step * n_f, tm, lambda r: s_copy(prv, r, start_prv).start())
        acc_ref[...] = jnp.zeros_like(acc_ref)

    for r in range(per_step):
        rl = f * per_step + r
        x_copy(rl, start_nxt).start()
        y_copy(cur, rl, start_cur).start()
        s_copy(prv, rl, start_prv).start()

    x = xb[...]
    hg = jnp.dot(x, wg_ref[...], preferred_element_type=F32)
    hu = jnp.dot(x, wu_ref[...], preferred_element_type=F32)
    hid = (jax.nn.silu(hg) * hu).astype(BF16)
    acc_ref[...] += jnp.dot(hid, wd_ref[...], preferred_element_type=F32)

    @pl.when(f == n_f - 1)
    def _():
        wait_rows(by.at[cur], sem_y.at[0])
        by[cur] = by[cur] + acc_ref[...] * gate_ref[...]

        @pl.when(t == n_t - 1)
        def _():
            for_rows(0, tm, lambda r: s_copy(cur, r, start_cur).start())
            wait_rows(by.at[prv], sem_s.at[prv])
            wait_rows(by.at[cur], sem_s.at[cur])
            wait_rows(bx, sem_x.at[0])


def _tile_order(n_experts, tiles_a, tiles_b):
    assert tiles_b == 2 * tiles_a
    order = []
    for e in range(n_experts):
        for i in range(tiles_a):
            order += [("a", e, i), ("b", e, 2 * i), ("b", e, 2 * i + 1)]
    return order


def _ffn(idx_a, idx_b, gate_a, gate_b, h, y, wg, wu, wd, tm=1024, tf=256):
    e, cap_a, _ = gate_a.shape
    cap_b = gate_b.shape[1]
    d = h.shape[1]
    ff = wg.shape[2]
    n_f = ff // tf
    order = _tile_order(e, cap_a // tm, cap_b // tm)
    off = {"a": 0, "b": e * cap_a}
    cap = {"a": cap_a, "b": cap_b}
    tstart = jnp.asarray([off[s] + ei * cap[s] + i * tm for s, ei, i in order], jnp.int32)
    texp = jnp.asarray([ei for _, ei, _ in order], jnp.int32)
    gates = {"a": gate_a.reshape(e, cap_a // tm, tm, 1), "b": gate_b.reshape(e, cap_b // tm, tm, 1)}
    gate = jnp.stack([gates[s][ei, i] for s, ei, i in order])
    idx = jnp.concatenate([idx_a, idx_b])
    grid_spec = pltpu.PrefetchScalarGridSpec(
        num_scalar_prefetch=3,
        grid=(len(order), n_f),
        in_specs=[
            pl.BlockSpec(memory_space=pl.ANY),
            pl.BlockSpec(memory_space=pl.ANY),
            pl.BlockSpec((None, tm, 1), lambda t, f, idx_r, ts_r, te_r: (t, 0, 0)),
            pl.BlockSpec((None, d, tf), lambda t, f, idx_r, ts_r, te_r: (te_r[t], 0, f)),
            pl.BlockSpec((None, d, tf), lambda t, f, idx_r, ts_r, te_r: (te_r[t], 0, f)),
            pl.BlockSpec((None, tf, d), lambda t, f, idx_r, ts_r, te_r: (te_r[t], f, 0)),
        ],
        out_specs=pl.BlockSpec(memory_space=pl.ANY),
        scratch_shapes=[
            pltpu.VMEM((tm, d), F32),
            pltpu.VMEM((2, tm, d), F32),
            pltpu.VMEM((tm, d), BF16),
            pltpu.VMEM((tm, d), F32),
            pltpu.SemaphoreType.DMA((1,)),
            pltpu.SemaphoreType.DMA((1,)),
            pltpu.SemaphoreType.DMA((2,)),
        ],
    )
    return pl.pallas_call(
        functools.partial(_ffn_kernel, tm=tm, n_f=n_f),
        grid_spec=grid_spec,
        out_shape=jax.ShapeDtypeStruct(y.shape, y.dtype),
        input_output_aliases={4: 0},
        compiler_params=pltpu.CompilerParams(dimension_semantics=("arbitrary", "arbitrary"),
                                             vmem_limit_bytes=FFN_VMEM_LIMIT),
        name="expert_ffn",
    )(idx, tstart, texp, h, y, gate, wg, wu, wd)


def _final_norm_kernel(x_ref, g_ref, o_ref):
    x = x_ref[...]
    ms = jnp.mean(x * x, axis=-1, keepdims=True)
    o_ref[...] = x * lax.rsqrt(ms + EPS) * g_ref[...]


def _final_norm(x, g, tm=1024):
    m, d = x.shape
    return pl.pallas_call(
        _final_norm_kernel,
        grid=(m // tm,),
        in_specs=[pl.BlockSpec((tm, d), lambda i: (i, 0)), pl.BlockSpec((1, d), lambda i: (0, 0))],
        out_specs=pl.BlockSpec((tm, d), lambda i: (i, 0)),
        out_shape=jax.ShapeDtypeStruct((m, d), F32),
        compiler_params=_params("parallel"),
        name="final_norm",
    )(x, g.reshape(1, d))


def _split_w_in(w):
    aw = ATTN_OUT_WIDTH
    part = lambda which, gi: w[:, which * ATTN_WIDTH + gi * aw:which * ATTN_WIDTH + (gi + 1) * aw]
    rest = 3 * ATTN_WIDTH
    w_main = jnp.concatenate(
        [w[:, rest:rest + POOL_WIDTH + SSM_WIDTH], part(0, 0), part(1, 0), part(2, 0),
         w[:, rest + POOL_WIDTH + SSM_WIDTH:]], axis=1).astype(BF16)
    w_dil = [jnp.concatenate([part(0, gi), part(1, gi), part(2, gi)], axis=1).astype(BF16)
             for gi in range(1, len(DILATED_GROUPS))]
    return w_main, w_dil


def _mixer(x, nb, norm_g, w_main, w_dil, pool_w, pool_scale, tables, d_skip, glu_w_t, wa, wp, ws, wo, norm2_g,
           router_w_t):
    proj, h = _inproj(x, norm_g, w_main)
    attn = [_attn_group(proj.reshape(nb, SEQ, MAIN_WIDTH), 0, nb, Q_OFF)]
    for gi in range(1, len(DILATED_GROUPS)):
        dil = DILATED_GROUPS[gi][1]
        qkv = _dilproj(h, w_dil[gi - 1], dil)
        attn.append(_attn_group(qkv.reshape(nb, SEQ // dil, dil * QKV_WIDTH), gi, nb, 0))
    attn_o = [a[0].reshape(-1, a[0].shape[2]) for a in attn]
    attn_l = [a[1].reshape(-1, a[1].shape[2]) for a in attn]
    pool_y = _pool(proj, pool_w, pool_scale, nb)

    u = proj[:, SSM_OFF:SSM_OFF + SSM_WIDTH].reshape(nb, N_CHUNKS, SSM_CHUNK, SSM_WIDTH)
    u_t = jnp.transpose(u, (3, 1, 0, 2)).reshape(SSM_WIDTH, N_CHUNKS * nb, SSM_CHUNK)
    y_t = _s5_mix(u_t, *tables, nb)
    ssm_y = _s5_glu(y_t.reshape(SSM_WIDTH, -1), u_t.reshape(SSM_WIDTH, -1), d_skip, glu_w_t, nb)

    merged = _merge(attn_o, attn_l, pool_y, ssm_y, proj, wa, wp, ws)
    return _outproj(merged, x, wo, norm2_g, router_w_t)


def _experts(y, h, aff_t, ta, tb, wg, wu, wd, tm=1024):
    idx_a, gate_a = _route(aff_t[:, :ta], CAPACITY_FACTOR * ta // N_EXPERTS, 0)
    idx_b, gate_b = _route(aff_t[:, ta:ta + tb], CAPACITY_FACTOR * tb // N_EXPERTS, ta)
    return _ffn(idx_a, idx_b, gate_a, gate_b, h, y, wg, wu, wd, tm=tm)


def kernel(x_prompt, x_sample, norm1_g, w_in, pool_w, pool_scale, ssm_a_re, ssm_a_im, ssm_log_step, ssm_b_re,
           ssm_b_im, ssm_c_re, ssm_c_im, ssm_d, ssm_glu_w, w_br_attn, w_br_pool, w_br_ssm, w_out, norm2_g,
           router_w, w_gate, w_up, w_down, final_norm_g):
    nbp, nbs = x_prompt.shape[0], x_sample.shape[0]
    nb = nbp + nbs
    tp, ts = nbp * SEQ, nbs * SEQ
    x = jnp.concatenate([x_prompt, x_sample], axis=0).reshape(nb * SEQ, D_MODEL)
    for l in range(DEPTH):
        w_main, w_dil = _split_w_in(w_in[l])
        tables = _s5_tables(ssm_a_re[l], ssm_a_im[l], ssm_log_step[l], ssm_b_re[l], ssm_b_im[l], ssm_c_re[l],
                            ssm_c_im[l])
        x, h, aff_t = _mixer(
            x, nb, norm1_g[l], w_main, w_dil, pool_w[l], pool_scale[l], tables, ssm_d[l],
            ssm_glu_w[l].T.astype(BF16), w_br_attn[l].astype(BF16), w_br_pool[l].astype(BF16),
            w_br_ssm[l].astype(BF16), w_out[l].astype(BF16), norm2_g[l], router_w[l].T)
        wg, wu, wd = w_gate[l].astype(BF16), w_up[l].astype(BF16), w_down[l].astype(BF16)
        x = _experts(x, h, aff_t, tp, ts, wg, wu, wd)
    y = _final_norm(x, final_norm_g).reshape(nb, SEQ, D_MODEL)
    return (y[:nbp], y[nbp:])
```

```python
import functools
import math

import jax
import jax.numpy as jnp
from jax import lax
from jax.experimental import pallas as pl
from jax.experimental.pallas import tpu as pltpu

F32 = jnp.float32
BF16 = jnp.bfloat16

D_MODEL = 2048
SEQ = 2048
DEPTH = 4
EPS = 1e-6
HEAD_DIM = 128
HEADS_PER_GROUP = 4
DILATED_GROUPS = ((128, 1), (512, 4), (2048, 16))
N_ATTN_HEADS = HEADS_PER_GROUP * len(DILATED_GROUPS)
ATTN_WIDTH = N_ATTN_HEADS * HEAD_DIM
ATTN_OUT_WIDTH = HEADS_PER_GROUP * HEAD_DIM
ALIBI_SLOPES = tuple(2.0 ** (-8.0 * (h + 1) / N_ATTN_HEADS) for h in range(N_ATTN_HEADS))
POOL_WINDOWS = (2, 4, 8, 16)
POOL_WIDTH = D_MODEL // 2
POOL_GROUP = POOL_WIDTH // len(POOL_WINDOWS)
SSM_WIDTH = D_MODEL // 2
SSM_GROUP = 16
SSM_GROUPS = SSM_WIDTH // SSM_GROUP
SSM_STATE = 64
IN_WIDTH = 3 * ATTN_WIDTH + POOL_WIDTH + SSM_WIDTH + 3 * D_MODEL
N_EXPERTS = 16
EXPERT_FF = 11 * D_MODEL // 8
CAPACITY_FACTOR = 2

POOL_OFF = 0
SSM_OFF = POOL_OFF + POOL_WIDTH
Q_OFF = SSM_OFF + SSM_WIDTH
K_OFF = Q_OFF + ATTN_OUT_WIDTH
V_OFF = K_OFF + ATTN_OUT_WIDTH
GATE_OFF = V_OFF + ATTN_OUT_WIDTH
MAIN_WIDTH = GATE_OFF + 3 * D_MODEL
QKV_WIDTH = 3 * ATTN_OUT_WIDTH

LANE = 128
SSM_CHUNK = LANE
N_CHUNKS = SEQ // SSM_CHUNK
ATTN_QB = 128
ATTN_HALF = 64
VMEM_LIMIT = 48 * 1024 * 1024
FFN_VMEM_LIMIT = 56 * 1024 * 1024


def _params(*sem):
    return pltpu.CompilerParams(dimension_semantics=sem, vmem_limit_bytes=VMEM_LIMIT)


def _inproj_kernel(x_ref, g_ref, w_ref, o_ref, h_ref):
    @pl.when(pl.program_id(1) == 0)
    def _():
        x = x_ref[...]
        ms = jnp.mean(x * x, axis=-1, keepdims=True)
        h_ref[...] = (x * lax.rsqrt(ms + EPS) * g_ref[...]).astype(BF16)

    o_ref[...] = jnp.dot(h_ref[...], w_ref[...], preferred_element_type=F32).astype(o_ref.dtype)


def _inproj(x, g, w, tm=1024, tn=512):
    m, d = x.shape
    n = w.shape[1]
    return pl.pallas_call(
        _inproj_kernel,
        grid=(m // tm, n // tn),
        in_specs=[
            pl.BlockSpec((tm, d), lambda i, j: (i, 0)),
            pl.BlockSpec((1, d), lambda i, j: (0, 0)),
            pl.BlockSpec((d, tn), lambda i, j: (0, j)),
        ],
        out_specs=[pl.BlockSpec((tm, tn), lambda i, j: (i, j)), pl.BlockSpec((tm, d), lambda i, j: (i, 0))],
        out_shape=[jax.ShapeDtypeStruct((m, n), BF16), jax.ShapeDtypeStruct((m, d), BF16)],
        compiler_params=_params("parallel", "arbitrary"),
        name="inproj",
    )(x, g.reshape(1, d), w)


def _dilproj_kernel(h_ref, w_ref, o_ref, y_scr, *, dil):
    rows = o_ref.shape[0]
    y = jnp.dot(h_ref[...], w_ref[...], preferred_element_type=F32)
    for c in range(QKV_WIDTH // LANE):
        y_scr[c] = y[:, c * LANE:(c + 1) * LANE]
    for r in range(dil):
        for c in range(QKV_WIDTH // LANE):
            col = r * QKV_WIDTH + c * LANE
            o_ref[:, col:col + LANE] = y_scr[c, pl.ds(r, rows, stride=dil), :].astype(o_ref.dtype)


def _dilproj(h, w, dil, tm=512):
    m, d = h.shape
    return pl.pallas_call(
        functools.partial(_dilproj_kernel, dil=dil),
        grid=(m // tm,),
        in_specs=[pl.BlockSpec((tm, d), lambda i: (i, 0)), pl.BlockSpec((d, QKV_WIDTH), lambda i: (0, 0))],
        out_specs=pl.BlockSpec((tm // dil, dil * QKV_WIDTH), lambda i: (i, 0)),
        out_shape=jax.ShapeDtypeStruct((m // dil, dil * QKV_WIDTH), BF16),
        scratch_shapes=[pltpu.VMEM((QKV_WIDTH // LANE, tm, LANE), F32)],
        compiler_params=_params("parallel"),
        name=f"dilproj_d{dil}",
    )(h, w)


def _attn_kernel(slope_ref, q_ref, k_ref, v_ref, o_ref, l_ref, *, n, dil):
    kw = min(2 * ATTN_QB, n)
    nqb = n // ATTN_QB
    scale = HEAD_DIM ** -0.5

    def body(j, carry):
        qs = pl.multiple_of(j * ATTN_QB, ATTN_QB)
        ks = pl.multiple_of(jnp.clip(qs - ATTN_HALF, 0, n - kw), ATTN_HALF)
        qpos = qs + lax.broadcasted_iota(jnp.int32, (ATTN_QB, kw), 0)
        kpos = ks + lax.broadcasted_iota(jnp.int32, (ATTN_QB, kw), 1)
        rel = jnp.abs(kpos - qpos)
        valid = rel <= ATTN_HALF
        relf = rel.astype(F32)
        for h in range(HEADS_PER_GROUP):
            cols = slice(h * HEAD_DIM, (h + 1) * HEAD_DIM)
            q = q_ref[pl.ds(qs, ATTN_QB), cols]
            k = k_ref[pl.ds(ks, kw), cols]
            v = v_ref[pl.ds(ks, kw), cols]
            s = lax.dot_general(q, k, (((1,), (1,)), ((), ())), preferred_element_type=F32) * scale
            neg_slope = -slope_ref[h][:, :kw] * float(dil)
            s = jnp.where(valid, s + neg_slope * relf, -jnp.inf)
            m = jnp.max(s, axis=1, keepdims=True)
            p = jnp.exp(s - m)
            l = jnp.sum(p, axis=1, keepdims=True)
            o = jnp.dot(p.astype(BF16), v, preferred_element_type=F32) / l
            o_ref[pl.ds(qs, ATTN_QB), cols] = o.astype(o_ref.dtype)
            l_ref[pl.ds(qs, ATTN_QB), cols] = jnp.broadcast_to(m + jnp.log(l), (ATTN_QB, HEAD_DIM))
        return carry

    lax.fori_loop(0, nqb, body, 0)


def _attn_group(qkv, gi, nb, col0):
    _, dil = DILATED_GROUPS[gi]
    n = SEQ // dil
    aw = ATTN_OUT_WIDTH
    cb = qkv.shape[2] // dil // aw
    slopes = jnp.asarray(ALIBI_SLOPES[gi * HEADS_PER_GROUP:(gi + 1) * HEADS_PER_GROUP], F32)
    slopes = jnp.broadcast_to(slopes[:, None, None], (HEADS_PER_GROUP, 1, 2 * ATTN_QB))

    def col(which):
        return lambda b, r: (b, 0, r * cb + col0 // aw + which)

    out_map = lambda b, r: (b, 0, r)
    return pl.pallas_call(
        functools.partial(_attn_kernel, n=n, dil=dil),
        grid=(nb, dil),
        in_specs=[
            pl.BlockSpec((HEADS_PER_GROUP, 1, 2 * ATTN_QB), lambda b, r: (0, 0, 0)),
            pl.BlockSpec((None, n, aw), col(0)),
            pl.BlockSpec((None, n, aw), col(1)),
            pl.BlockSpec((None, n, aw), col(2)),
        ],
        out_specs=[pl.BlockSpec((None, n, aw), out_map), pl.BlockSpec((None, n, aw), out_map)],
        out_shape=[
            jax.ShapeDtypeStruct((nb, n, dil * aw), BF16),
            jax.ShapeDtypeStruct((nb, n, dil * aw), F32),
        ],
        compiler_params=_params("parallel", "parallel"),
        name=f"attn_g{gi}",
    )(slopes, qkv, qkv, qkv)


def _shift_rows(x, k, row):
    n = x.shape[0]
    if k > 0:
        return jnp.where(row >= k, pltpu.roll(x, k, 0), 0.0)
    return jnp.where(row < n + k, pltpu.roll(x, n + k, 0), 0.0)


def _pool_kernel(u_ref, w_ref, s_ref, o_ref):
    n = u_ref.shape[0]
    row = lax.broadcasted_iota(jnp.int32, (n, POOL_GROUP), 0)
    for gi, win in enumerate(POOL_WINDOWS):
        half = win // 2
        cols = slice(gi * POOL_GROUP, (gi + 1) * POOL_GROUP)
        u = u_ref[:, cols].astype(F32)
        back = _shift_rows(u, 1, row)
        fwd = u
        step = 1
        while step < half:
            back = back + _shift_rows(back, step, row)
            fwd = fwd + _shift_rows(fwd, -step, row)
            step *= 2
        t = row[:, :1]
        lo = jnp.clip(t - half, 0, n - 1)
        hi = jnp.clip(t + half - 1, 0, n - 1)
        cnt = (hi - lo + 1).astype(F32)
        pooled = (back + fwd) / cnt - u
        y = jnp.dot(pooled.astype(BF16), w_ref[gi], preferred_element_type=F32)
        o_ref[:, cols] = (y * s_ref[:, cols]).astype(o_ref.dtype)


def _pool(proj, pool_w, pool_scale, nb):
    p3 = proj.reshape(nb, SEQ, MAIN_WIDTH)
    out = pl.pallas_call(
        _pool_kernel,
        grid=(nb,),
        in_specs=[
            pl.BlockSpec((None, SEQ, POOL_WIDTH), lambda b: (b, 0, POOL_OFF // POOL_WIDTH)),
            pl.BlockSpec((len(POOL_WINDOWS), POOL_GROUP, POOL_GROUP), lambda b: (0, 0, 0)),
            pl.BlockSpec((1, POOL_WIDTH), lambda b: (0, 0)),
        ],
        out_specs=pl.BlockSpec((None, SEQ, POOL_WIDTH), lambda b: (b, 0, 0)),
        out_shape=jax.ShapeDtypeStruct((nb, SEQ, POOL_WIDTH), BF16),
        compiler_params=_params("parallel"),
        name="pool",
    )(p3, pool_w.astype(BF16), pool_scale.reshape(1, POOL_WIDTH))
    return out.reshape(nb * SEQ, POOL_WIDTH)


def _s5_tables(a_re, a_im, log_step, b_re, b_im, c_re, c_im):
    t_len = SSM_CHUNK
    hp = lax.Precision.HIGHEST
    lr, li = a_re.astype(F32), a_im.astype(F32)
    dt = jnp.exp(log_step.astype(F32))[..., None]
    mag = jnp.exp(lr * dt)
    ab_r, ab_i = mag * jnp.cos(li * dt), mag * jnp.sin(li * dt)
    den = lr * lr + li * li
    xr, xi = ab_r - 1.0, ab_i
    co_r = (xr * lr + xi * li) / den
    co_i = (xi * lr - xr * li) / den
    br, bi = b_re.astype(F32), b_im.astype(F32)
    bb_r = co_r[..., None] * br - co_i[..., None] * bi
    bb_i = co_r[..., None] * bi + co_i[..., None] * br
    cr, ci = c_re.astype(F32), c_im.astype(F32)

    tau = jnp.arange(t_len + 1, dtype=F32)
    pmag = jnp.exp((lr * dt)[..., None] * tau)
    pang = (li * dt)[..., None] * tau
    pw_r, pw_i = pmag * jnp.cos(pang), pmag * jnp.sin(pang)

    cb_r = cr[:, :, :, None, :] * jnp.swapaxes(bb_r, 2, 3)[:, :, None, :, :] \
        - ci[:, :, :, None, :] * jnp.swapaxes(bb_i, 2, 3)[:, :, None, :, :]
    cb_i = cr[:, :, :, None, :] * jnp.swapaxes(bb_i, 2, 3)[:, :, None, :, :] \
        + ci[:, :, :, None, :] * jnp.swapaxes(bb_r, 2, 3)[:, :, None, :, :]
    kern = jnp.einsum('dgabp,dgpt->dgabt', cb_r, pw_r[..., :t_len], precision=hp) \
        - jnp.einsum('dgabp,dgpt->dgabt', cb_i, pw_i[..., :t_len], precision=hp)
    kf, kb = kern[0], kern[1]
    zero = jnp.zeros_like(kf[..., :1])
    kvec = jnp.concatenate([zero, kb[..., :0:-1], kf[..., :1] + kb[..., :1], kf[..., 1:]], axis=-1)
    kvec = jnp.swapaxes(kvec, 1, 2).reshape(SSM_GROUPS, SSM_GROUP * SSM_GROUP, 2 * t_len)

    def cmul(ar, ai, xr_, xi_):
        return ar * xr_ - ai * xi_, ar * xi_ + ai * xr_

    pf_r, pf_i = pw_r[0][..., t_len - 1::-1][..., :t_len], pw_i[0][..., t_len - 1::-1][..., :t_len]
    pb_r, pb_i = pw_r[1][..., :t_len], pw_i[1][..., :t_len]
    ef_r, ef_i = cmul(pf_r[:, :, None, :], pf_i[:, :, None, :], bb_r[0][..., None], bb_i[0][..., None])
    eb_r, eb_i = cmul(pb_r[:, :, None, :], pb_i[:, :, None, :], bb_r[1][..., None], bb_i[1][..., None])
    to_e = lambda a: jnp.transpose(a, (0, 2, 3, 1)).reshape(SSM_GROUPS, SSM_GROUP * t_len, SSM_STATE)
    e_tab = jnp.concatenate([to_e(ef_r), to_e(ef_i), to_e(eb_r), to_e(eb_i)], axis=-1)

    qf_r, qf_i = pw_r[0][..., 1:], pw_i[0][..., 1:]
    qb_r, qb_i = pw_r[1][..., :0:-1], pw_i[1][..., :0:-1]
    wf_r, wf_i = cmul(jnp.swapaxes(cr[0], 1, 2)[..., None], jnp.swapaxes(ci[0], 1, 2)[..., None],
                      qf_r[:, :, None, :], qf_i[:, :, None, :])
    wb_r, wb_i = cmul(jnp.swapaxes(cr[1], 1, 2)[..., None], jnp.swapaxes(ci[1], 1, 2)[..., None],
                      qb_r[:, :, None, :], qb_i[:, :, None, :])
    to_f = lambda a: a.reshape(SSM_GROUPS, SSM_STATE, SSM_GROUP * t_len)
    f_tab = jnp.concatenate([to_f(wf_r), to_f(-wf_i), to_f(wb_r), to_f(-wb_i)], axis=1)

    at_r, at_i = pw_r[..., t_len], pw_i[..., t_len]
    row0 = jnp.concatenate([at_r[0], at_r[0], at_r[1], at_r[1]], axis=-1)
    row1 = jnp.concatenate([-at_i[0], at_i[0], -at_i[1], at_i[1]], axis=-1)
    dec = jnp.concatenate([row0[:, None, :], row1[:, None, :], jnp.zeros((SSM_GROUPS, 6, 4 * SSM_STATE), F32)], axis=1)
    return kvec, e_tab.astype(BF16), f_tab.astype(BF16), dec


def _s5_kernel(u_ref, kvec_ref, e_ref, f_ref, dec_ref, d_ref, y_ref, m_scr, hc_scr, *, nb):
    t_len = SSM_CHUNK
    h = SSM_GROUP

    def build(hi, carry):
        for ho in range(h):
            row = kvec_ref[pl.ds(hi * h + ho, 1), :]
            rolled = pltpu.roll(jnp.broadcast_to(row, (t_len, 2 * t_len)), 0, 1, stride=1, stride_axis=0)
            m_scr[pl.ds(pl.multiple_of(hi * t_len, t_len), t_len), ho * t_len:(ho + 1) * t_len] = (
                rolled[:, t_len:].astype(BF16))
        return carry

    lax.fori_loop(0, h, build, 0)

    x = jnp.concatenate([u_ref[hi] for hi in range(h)], axis=1)
    s = jnp.dot(x, e_ref[...], preferred_element_type=F32)

    d0, d1 = dec_ref[0:1, :], dec_ref[1:2, :]
    for half, order in ((0, range(N_CHUNKS)), (1, range(N_CHUNKS - 1, -1, -1))):
        lanes = slice(half * 2 * SSM_STATE, (half + 1) * 2 * SSM_STATE)
        a0, a1 = d0[:, lanes], d1[:, lanes]
        state = jnp.zeros((nb, 2 * SSM_STATE), F32)
        for c in order:
            rows = slice(c * nb, (c + 1) * nb)
            hc_scr[rows, lanes] = state
            state = a0 * state + a1 * pltpu.roll(state, SSM_STATE, 1) + s[rows, lanes]

    hc = hc_scr[...]
    hc_hi = hc.astype(BF16)
    hc_lo = (hc - hc_hi.astype(F32)).astype(BF16)
    for ho in range(0, h, 2):
        cols = slice(ho * t_len, (ho + 2) * t_len)
        y = jnp.dot(x, m_scr[:, cols], preferred_element_type=F32)
        y = y + jnp.dot(hc_hi, f_ref[:, cols], preferred_element_type=F32)
        y = y + jnp.dot(hc_lo, f_ref[:, cols], preferred_element_type=F32)
        y_ref[ho] = y[:, :t_len] + d_ref[ho] * u_ref[ho].astype(F32)
        y_ref[ho + 1] = y[:, t_len:] + d_ref[ho + 1] * u_ref[ho + 1].astype(F32)


def _s5_mix(u_t, kvec, e_tab, f_tab, dec, d_skip, nb):
    rows = N_CHUNKS * nb
    tl = SSM_CHUNK
    d_b = jnp.broadcast_to(d_skip.astype(F32)[:, None, None], (SSM_WIDTH, 1, tl))
    return pl.pallas_call(
        functools.partial(_s5_kernel, nb=nb),
        grid=(SSM_GROUPS,),
        in_specs=[
            pl.BlockSpec((SSM_GROUP, rows, tl), lambda g: (g, 0, 0)),
            pl.BlockSpec((None, SSM_GROUP * SSM_GROUP, 2 * tl), lambda g: (g, 0, 0)),
            pl.BlockSpec((None, SSM_GROUP * tl, 4 * SSM_STATE), lambda g: (g, 0, 0)),
            pl.BlockSpec((None, 4 * SSM_STATE, SSM_GROUP * tl), lambda g: (g, 0, 0)),
            pl.BlockSpec((None, 8, 4 * SSM_STATE), lambda g: (g, 0, 0)),
            pl.BlockSpec((SSM_GROUP, 1, tl), lambda g: (g, 0, 0)),
        ],
        out_specs=pl.BlockSpec((SSM_GROUP, rows, tl), lambda g: (g, 0, 0)),
        out_shape=jax.ShapeDtypeStruct((SSM_WIDTH, rows, tl), F32),
        scratch_shapes=[
            pltpu.VMEM((SSM_GROUP * tl, SSM_GROUP * tl), BF16),
            pltpu.VMEM((rows, 4 * SSM_STATE), F32),
        ],
        compiler_params=_params("parallel"),
        name="s5_mix",
    )(u_t, kvec, e_tab, f_tab, dec, d_b)


def _glu_kernel(y_ref, w_ref, o_ref):
    tl = o_ref.shape[1]
    g = jax.nn.gelu(y_ref[...])
    z = jnp.dot(w_ref[...], g.astype(BF16), preferred_element_type=F32)
    o = g * jax.nn.sigmoid(z)
    for j in range(o_ref.shape[0]):
        o_ref[j] = o[:, j * tl:(j + 1) * tl].T.astype(o_ref.dtype)


GLU_BATCHES = 8


def _s5_glu(y_t, glu_w_t, nb):
    tl = SSM_CHUNK
    kb = min(GLU_BATCHES, nb)
    per_chunk = nb // kb
    return pl.pallas_call(
        _glu_kernel,
        grid=(N_CHUNKS * per_chunk,),
        in_specs=[
            pl.BlockSpec((SSM_WIDTH, kb * tl), lambda i: (0, i)),
            pl.BlockSpec((SSM_WIDTH, SSM_WIDTH), lambda i: (0, 0)),
        ],
        out_specs=pl.BlockSpec((kb, None, tl, SSM_WIDTH), lambda i: (i % per_chunk, i // per_chunk, 0, 0)),
        out_shape=jax.ShapeDtypeStruct((nb, N_CHUNKS, tl, SSM_WIDTH), BF16),
        compiler_params=_params("parallel"),
        name="s5_glu",
    )(y_t, glu_w_t)


def _merge_kernel(o1, o2, o3, l1, l2, l3, pool_ref, ssm_ref, ga, gp, gs, wa, wp, ws, out_ref, attn_scr, os_scr, ls_scr):
    @pl.when(pl.program_id(1) == 0)
    def _():
        aw = ATTN_OUT_WIDTH
        for gi, (o_ref, l_ref) in enumerate(((o2, l2), (o3, l3))):
            dil = DILATED_GROUPS[gi + 1][1]
            rows = o_ref.shape[0]
            for r in range(dil):
                for c in range(HEADS_PER_GROUP):
                    col = r * aw + c * HEAD_DIM
                    os_scr[gi, c, pl.ds(r, rows, stride=dil), :] = o_ref[:, col:col + HEAD_DIM].astype(F32)
                    ls_scr[gi, c, pl.ds(r, rows, stride=dil), :] = l_ref[:, col:col + HEAD_DIM]
        for c in range(HEADS_PER_GROUP):
            cols = slice(c * HEAD_DIM, (c + 1) * HEAD_DIM)
            a1, a2, a3 = l1[:, cols], ls_scr[0, c], ls_scr[1, c]
            m = jnp.maximum(jnp.maximum(a1, a2), a3)
            e1, e2, e3 = jnp.exp(a1 - m), jnp.exp(a2 - m), jnp.exp(a3 - m)
            inv = 1.0 / (e1 + e2 + e3)
            y = o1[:, cols].astype(F32) * (e1 * inv) + os_scr[0, c] * (e2 * inv) + os_scr[1, c] * (e3 * inv)
            attn_scr[:, cols] = y.astype(BF16)

    ya = jnp.dot(attn_scr[...], wa[...], preferred_element_type=F32)
    yp = jnp.dot(pool_ref[...], wp[...], preferred_element_type=F32)
    ys = jnp.dot(ssm_ref[...], ws[...], preferred_element_type=F32)
    merged = (jax.nn.sigmoid(ga[...].astype(F32)) * ya + jax.nn.sigmoid(gp[...].astype(F32)) * yp
              + jax.nn.sigmoid(gs[...].astype(F32)) * ys)
    out_ref[...] = merged.astype(out_ref.dtype)


def _merge(os_, ls_, pool_y, ssm_y, proj, wa, wp, ws, tm=512, tn=512):
    m = proj.shape[0]
    gb = GATE_OFF // tn
    nj = D_MODEL // tn
    aw = ATTN_OUT_WIDTH
    row = lambda i, j: (i, 0)
    dils = [d for _, d in DILATED_GROUPS]
    attn_specs = [pl.BlockSpec((tm // d, d * aw), row) for d in dils]
    return pl.pallas_call(
        _merge_kernel,
        grid=(m // tm, nj),
        in_specs=attn_specs + attn_specs + [
            pl.BlockSpec((tm, POOL_WIDTH), row),
            pl.BlockSpec((tm, SSM_WIDTH), row),
            pl.BlockSpec((tm, tn), lambda i, j: (i, gb + j)),
            pl.BlockSpec((tm, tn), lambda i, j: (i, gb + nj + j)),
            pl.BlockSpec((tm, tn), lambda i, j: (i, gb + 2 * nj + j)),
            pl.BlockSpec((ATTN_OUT_WIDTH, tn), lambda i, j: (0, j)),
            pl.BlockSpec((POOL_WIDTH, tn), lambda i, j: (0, j)),
            pl.BlockSpec((SSM_WIDTH, tn), lambda i, j: (0, j)),
        ],
        out_specs=pl.BlockSpec((tm, tn), lambda i, j: (i, j)),
        out_shape=jax.ShapeDtypeStruct((m, D_MODEL), BF16),
        scratch_shapes=[
            pltpu.VMEM((tm, aw), BF16),
            pltpu.VMEM((len(dils) - 1, HEADS_PER_GROUP, tm, HEAD_DIM), F32),
            pltpu.VMEM((len(dils) - 1, HEADS_PER_GROUP, tm, HEAD_DIM), F32),
        ],
        compiler_params=_params("parallel", "arbitrary"),
        name="merge",
    )(*os_, *ls_, pool_y, ssm_y, proj, proj, proj, wa, wp, ws)


def _outproj_kernel(mg_ref, x_ref, wo_ref, g_ref, rw_ref, xo_ref, h_ref, aff_ref):
    xn = x_ref[...] + jnp.dot(mg_ref[...], wo_ref[...], preferred_element_type=F32)
    xo_ref[...] = xn
    ms = jnp.mean(xn * xn, axis=-1, keepdims=True)
    h = xn * lax.rsqrt(ms + EPS) * g_ref[...]
    h_ref[...] = h.astype(h_ref.dtype)
    logits = lax.dot_general(rw_ref[...], h, (((1,), (1,)), ((), ())), preferred_element_type=F32,
                             precision=lax.Precision.HIGHEST)
    mx = jnp.max(logits, axis=0, keepdims=True)
    ex = jnp.exp(logits - mx)
    aff_ref[...] = ex / jnp.sum(ex, axis=0, keepdims=True)


def _outproj(merged, x, wo, g2, router_w_t, tm=512):
    m = x.shape[0]
    return pl.pallas_call(
        _outproj_kernel,
        grid=(m // tm,),
        in_specs=[
            pl.BlockSpec((tm, D_MODEL), lambda i: (i, 0)),
            pl.BlockSpec((tm, D_MODEL), lambda i: (i, 0)),
            pl.BlockSpec((D_MODEL, D_MODEL), lambda i: (0, 0)),
            pl.BlockSpec((1, D_MODEL), lambda i: (0, 0)),
            pl.BlockSpec((N_EXPERTS, D_MODEL), lambda i: (0, 0)),
        ],
        out_specs=[
            pl.BlockSpec((tm, D_MODEL), lambda i: (i, 0)),
            pl.BlockSpec((tm, D_MODEL), lambda i: (i, 0)),
            pl.BlockSpec((N_EXPERTS, tm), lambda i: (0, i)),
        ],
        out_shape=[
            jax.ShapeDtypeStruct((m, D_MODEL), F32),
            jax.ShapeDtypeStruct((m, D_MODEL), F32),
            jax.ShapeDtypeStruct((N_EXPERTS, m), F32),
        ],
        compiler_params=_params("parallel"),
        name="outproj",
    )(merged, x, wo, g2.reshape(1, D_MODEL), router_w_t)


ROUTE_CHUNK = 1024


def _lane_cumsum(m, inc_scr, tot_scr, off_scr):
    e_count, nt, lanes = m.shape
    m2 = m.reshape(e_count * nt, lanes).astype(BF16)
    r = lax.broadcasted_iota(jnp.int32, (lanes, lanes), 0)
    c = lax.broadcasted_iota(jnp.int32, (lanes, lanes), 1)
    upper = jnp.where(r <= c, 1.0, 0.0).astype(BF16)
    ones = jnp.ones((lanes, lanes), BF16)
    inc_scr[...] = jnp.dot(m2, upper, preferred_element_type=F32).reshape(e_count, nt, lanes)
    tot_scr[...] = jnp.dot(m2, ones, preferred_element_type=F32).reshape(e_count, nt, lanes)
    rr = lax.broadcasted_iota(jnp.int32, (nt, nt), 0)
    cc = lax.broadcasted_iota(jnp.int32, (nt, nt), 1)
    lower = jnp.where(cc < rr, 1.0, 0.0).astype(BF16)
    for e in range(e_count):
        off_scr[e] = jnp.dot(lower, tot_scr[e].astype(BF16), preferred_element_type=F32)


def _route_kernel(aff_ref, idx_ref, gate_ref, inc_scr, tot_scr, off_scr, *, cap, base):
    e_count, nt, lanes = aff_ref.shape
    e = pl.program_id(0)

    @pl.when(e == 0)
    def _():
        bits = lax.bitcast_convert_type(aff_ref[...], jnp.int32)

        def search(b, cur):
            cand = cur | jnp.left_shift(jnp.int32(1), 30 - b)
            cnt = jnp.sum(jnp.where(bits >= cand, 1.0, 0.0), axis=(1, 2), keepdims=True)
            return jnp.where(cnt >= cap, cand, cur)

        thr = lax.fori_loop(0, 31, search, jnp.zeros((e_count, 1, 1), jnp.int32))
        gt = bits > thr
        eq = bits == thr
        need = cap - jnp.sum(jnp.where(gt, 1.0, 0.0), axis=(1, 2), keepdims=True)
        eqf = jnp.where(eq, 1.0, 0.0)
        _lane_cumsum(eqf, inc_scr, tot_scr, off_scr)
        rank = off_scr[...] + inc_scr[...] - eqf
        sel = jnp.where(gt, 1.0, jnp.where(eq & (rank < need), 1.0, 0.0))
        _lane_cumsum(sel, inc_scr, tot_scr, off_scr)

    inc = inc_scr[e].astype(BF16)
    off = off_scr[e]
    cend_row = jnp.transpose(off + tot_scr[e])[0:1, :]
    off_row = jnp.transpose(off)[0:1, :]
    aff_e = aff_ref[e]
    ch = min(ROUTE_CHUNK, cap)
    row_lane = lax.broadcasted_iota(jnp.int32, (ch, nt), 1).astype(F32)
    lane = lax.broadcasted_iota(jnp.int32, (ch, lanes), 1).astype(F32)

    def chunk(ci, carry):
        s0 = pl.multiple_of(ci * ch, ch)
        slot = (s0 + lax.broadcasted_iota(jnp.int32, (ch, 1), 0)).astype(F32)
        krow = jnp.sum(jnp.where(cend_row <= slot, 1.0, 0.0), axis=1, keepdims=True)
        hit = row_lane == krow
        onehot = jnp.where(hit, 1.0, 0.0)
        loc = jnp.dot(onehot.astype(BF16), inc, preferred_element_type=F32)
        target = slot - jnp.sum(jnp.where(hit, off_row, 0.0), axis=1, keepdims=True)
        l = jnp.sum(jnp.where(loc <= target, 1.0, 0.0), axis=1, keepdims=True)
        aff_row = jnp.dot(onehot, aff_e, preferred_element_type=F32, precision=lax.Precision.HIGHEST)
        gate_ref[pl.ds(s0, ch), :] = jnp.sum(jnp.where(lane == l, aff_row, 0.0), axis=1, keepdims=True)
        idx_ref[pl.ds(s0, ch), :] = (krow * lanes + l).astype(jnp.int32) + base
        return carry

    lax.fori_loop(0, cap // ch, chunk, 0)


def _route(aff_t, cap, base):
    e_count, t = aff_t.shape
    nt = t // LANE
    idx, gate = pl.pallas_call(
        functools.partial(_route_kernel, cap=cap, base=base),
        grid=(e_count,),
        in_specs=[pl.BlockSpec((e_count, nt, LANE), lambda e: (0, 0, 0))],
        out_specs=[pl.BlockSpec((None, cap, 1), lambda e: (e, 0, 0)), pl.BlockSpec((None, cap, 1), lambda e: (e, 0, 0))],
        out_shape=[jax.ShapeDtypeStruct((e_count, cap, 1), jnp.int32), jax.ShapeDtypeStruct((e_count, cap, 1), F32)],
        scratch_shapes=[pltpu.VMEM((e_count, nt, LANE), F32)] * 3,
        compiler_params=_params("arbitrary"),
        name="route",
    )(aff_t.reshape(e_count, nt, LANE))
    return idx.reshape(e_count * cap), gate


def _ffn_kernel(idx_ref, tstart_ref, texp_ref, h_hbm, y_hbm, gate_ref, wg_ref, wu_ref, wd_ref, yo_hbm, bx, by, xb,
                acc_ref, sem_x, sem_y, sem_s, *, tm, n_f):
    del y_hbm, texp_ref
    t, f = pl.program_id(0), pl.program_id(1)
    n_t = pl.num_programs(0)
    cur = t % 2
    prv = 1 - cur
    per_step = tm // n_f
    start_cur = tstart_ref[t]
    start_nxt = tstart_ref[jnp.minimum(t + 1, n_t - 1)]
    start_prv = tstart_ref[jnp.where(t == 0, n_t - 1, t - 1)]

    def x_copy(r, start):
        row = idx_ref[start + r]
        return pltpu.make_async_copy(h_hbm.at[pl.ds(row, 1), :], bx.at[pl.ds(r, 1), :], sem_x.at[0])

    def y_copy(slot, r, start):
        row = idx_ref[start + r]
        return pltpu.make_async_copy(yo_hbm.at[pl.ds(row, 1), :], by.at[slot, pl.ds(r, 1), :], sem_y.at[0])

    def s_copy(slot, r, start):
        row = idx_ref[start + r]
        return pltpu.make_async_copy(by.at[slot, pl.ds(r, 1), :], yo_hbm.at[pl.ds(row, 1), :], sem_s.at[slot])

    def for_rows(lo, hi, fn):
        def body(r, carry):
            fn(r)
            return carry

        lax.fori_loop(lo, hi, body, 0)

    def wait_rows(dst, sem):
        pltpu.make_async_copy(h_hbm.at[pl.ds(0, tm), :], dst, sem).wait()

    @pl.when(f == 0)
    def _():
        @pl.when(t == 0)
        def _():
            for_rows(0, tm, lambda r: x_copy(r, start_cur).start())
            for_rows(0, tm, lambda r: y_copy(prv, r, start_prv).start())
            wait_rows(by.at[prv], sem_y.at[0])

        wait_rows(bx, sem_x.at[0])
        xb[...] = bx[...].astype(BF16)

        @pl.when(t >= 1)
        def _():
            wait_rows(by.at[cur], sem_s.at[cur])

        for_rows(per_step * n_f, tm, lambda r: x_copy(r, start_nxt).start())
        for_rows(per_step * n_f, tm, lambda r: y_copy(cur, r, start_cur).start())
        for_rows(per_step * n_f, tm, lambda r: s_copy(prv, r, start_prv).start())
        acc_ref[...] = jnp.zeros_like(acc_ref)

    for r in range(per_step):
        rl = f * per_step + r
        x_copy(rl, start_nxt).start()
        y_copy(cur, rl, start_cur).start()
        s_copy(prv, rl, start_prv).start()

    x = xb[...]
    hg = jnp.dot(x, wg_ref[...], preferred_element_type=F32)
    hu = jnp.dot(x, wu_ref[...], preferred_element_type=F32)
    hid = (jax.nn.silu(hg) * hu).astype(BF16)
    acc_ref[...] += jnp.dot(hid, wd_ref[...], preferred_element_type=F32)

    @pl.when(f == n_f - 1)
    def _():
        wait_rows(by.at[cur], sem_y.at[0])
        by[cur] = by[cur] + acc_ref[...] * gate_ref[...]

        @pl.when(t == n_t - 1)
        def _():
            for_rows(0, tm, lambda r: s_copy(cur, r, start_cur).start())
            wait_rows(by.at[prv], sem_s.at[prv])
            wait_rows(by.at[cur], sem_s.at[cur])
            wait_rows(bx, sem_x.at[0])


def _tile_order(n_experts, tiles_a, tiles_b):
    assert tiles_b == 2 * tiles_a
    order = []
    for e in range(n_experts):
        for i in range(tiles_a):
            order += [("a", e, i), ("b", e, 2 * i), ("b", e, 2 * i + 1)]
    return order


def _ffn(idx_a, idx_b, gate_a, gate_b, h, y, wg, wu, wd, tm=1024, tf=256):
    e, cap_a, _ = gate_a.shape
    cap_b = gate_b.shape[1]
    d = h.shape[1]
    ff = wg.shape[2]
    n_f = ff // tf
    order = _tile_order(e, cap_a // tm, cap_b // tm)
    off = {"a": 0, "b": e * cap_a}
    cap = {"a": cap_a, "b": cap_b}
    tstart = jnp.asarray([off[s] + ei * cap[s] + i * tm for s, ei, i in order], jnp.int32)
    texp = jnp.asarray([ei for _, ei, _ in order], jnp.int32)
    gate = jnp.concatenate([gate_a.reshape(e, cap_a // tm, 1, tm, 1), gate_b.reshape(e, cap_a // tm, 2, tm, 1)],
                           axis=2).reshape(len(order), tm, 1)
    idx = jnp.concatenate([idx_a, idx_b])
    grid_spec = pltpu.PrefetchScalarGridSpec(
        num_scalar_prefetch=3,
        grid=(len(order), n_f),
        in_specs=[
            pl.BlockSpec(memory_space=pl.ANY),
            pl.BlockSpec(memory_space=pl.ANY),
            pl.BlockSpec((None, tm, 1), lambda t, f, idx_r, ts_r, te_r: (t, 0, 0)),
            pl.BlockSpec((None, d, tf), lambda t, f, idx_r, ts_r, te_r: (te_r[t], 0, f)),
            pl.BlockSpec((None, d, tf), lambda t, f, idx_r, ts_r, te_r: (te_r[t], 0, f)),
            pl.BlockSpec((None, tf, d), lambda t, f, idx_r, ts_r, te_r: (te_r[t], f, 0)),
        ],
        out_specs=pl.BlockSpec(memory_space=pl.ANY),
        scratch_shapes=[
            pltpu.VMEM((tm, d), F32),
            pltpu.VMEM((2, tm, d), F32),
            pltpu.VMEM((tm, d), BF16),
            pltpu.VMEM((tm, d), F32),
            pltpu.SemaphoreType.DMA((1,)),
            pltpu.SemaphoreType.DMA((1,)),
            pltpu.SemaphoreType.DMA((2,)),
        ],
    )
    return pl.pallas_call(
        functools.partial(_ffn_kernel, tm=tm, n_f=n_f),
        grid_spec=grid_spec,
        out_shape=jax.ShapeDtypeStruct(y.shape, y.dtype),
        input_output_aliases={4: 0},
        compiler_params=pltpu.CompilerParams(dimension_semantics=("arbitrary", "arbitrary"),
                                             vmem_limit_bytes=FFN_VMEM_LIMIT),
        name="expert_ffn",
    )(idx, tstart, texp, h, y, gate, wg, wu, wd)


def _final_norm_kernel(x_ref, g_ref, o_ref):
    x = x_ref[...]
    ms = jnp.mean(x * x, axis=-1, keepdims=True)
    o_ref[...] = x * lax.rsqrt(ms + EPS) * g_ref[...]


def _final_norm(x, g, row0, rows, tm=1024):
    d = x.shape[1]
    first = row0 // tm
    return pl.pallas_call(
        _final_norm_kernel,
        grid=(rows // tm,),
        in_specs=[pl.BlockSpec((tm, d), lambda i: (first + i, 0)), pl.BlockSpec((1, d), lambda i: (0, 0))],
        out_specs=pl.BlockSpec((tm, d), lambda i: (i, 0)),
        out_shape=jax.ShapeDtypeStruct((rows, d), F32),
        compiler_params=_params("parallel"),
        name="final_norm",
    )(x, g.reshape(1, d))


def _split_w_in(w):
    aw = ATTN_OUT_WIDTH
    part = lambda which, gi: w[:, which * ATTN_WIDTH + gi * aw:which * ATTN_WIDTH + (gi + 1) * aw]
    rest = 3 * ATTN_WIDTH
    w_main = jnp.concatenate(
        [w[:, rest:rest + POOL_WIDTH + SSM_WIDTH], part(0, 0), part(1, 0), part(2, 0),
         w[:, rest + POOL_WIDTH + SSM_WIDTH:]], axis=1).astype(BF16)
    w_dil = [jnp.concatenate([part(0, gi), part(1, gi), part(2, gi)], axis=1).astype(BF16)
             for gi in range(1, len(DILATED_GROUPS))]
    return w_main, w_dil


def _mixer(x, nb, norm_g, w_main, w_dil, pool_w, pool_scale, tables, d_skip, glu_w_t, wa, wp, ws, wo, norm2_g,
           router_w_t):
    proj, h = _inproj(x, norm_g, w_main)
    attn = [_attn_group(proj.reshape(nb, SEQ, MAIN_WIDTH), 0, nb, Q_OFF)]
    for gi in range(1, len(DILATED_GROUPS)):
        dil = DILATED_GROUPS[gi][1]
        qkv = _dilproj(h, w_dil[gi - 1], dil)
        attn.append(_attn_group(qkv.reshape(nb, SEQ // dil, dil * QKV_WIDTH), gi, nb, 0))
    attn_o = [a[0].reshape(-1, a[0].shape[2]) for a in attn]
    attn_l = [a[1].reshape(-1, a[1].shape[2]) for a in attn]
    pool_y = _pool(proj, pool_w, pool_scale, nb)

    u = proj[:, SSM_OFF:SSM_OFF + SSM_WIDTH].reshape(nb, N_CHUNKS, SSM_CHUNK, SSM_WIDTH)
    u_t = jnp.transpose(u, (3, 1, 0, 2)).reshape(SSM_WIDTH, N_CHUNKS * nb, SSM_CHUNK)
    y_t = _s5_mix(u_t, *tables, d_skip, nb)
    ssm_y = _s5_glu(y_t.reshape(SSM_WIDTH, -1), glu_w_t, nb).reshape(nb * SEQ, SSM_WIDTH)

    merged = _merge(attn_o, attn_l, pool_y, ssm_y, proj, wa, wp, ws)
    return _outproj(merged, x, wo, norm2_g, router_w_t)


def _experts(y, h, aff_t, ta, tb, wg, wu, wd, tm=1024):
    idx_a, gate_a = _route(aff_t[:, :ta], CAPACITY_FACTOR * ta // N_EXPERTS, 0)
    idx_b, gate_b = _route(aff_t[:, ta:ta + tb], CAPACITY_FACTOR * tb // N_EXPERTS, ta)
    return _ffn(idx_a, idx_b, gate_a, gate_b, h, y, wg, wu, wd, tm=tm)


def kernel(x_prompt, x_sample, norm1_g, w_in, pool_w, pool_scale, ssm_a_re, ssm_a_im, ssm_log_step, ssm_b_re,
           ssm_b_im, ssm_c_re, ssm_c_im, ssm_d, ssm_glu_w, w_br_attn, w_br_pool, w_br_ssm, w_out, norm2_g,
           router_w, w_gate, w_up, w_down, final_norm_g):
    nbp, nbs = x_prompt.shape[0], x_sample.shape[0]
    nb = nbp + nbs
    tp, ts = nbp * SEQ, nbs * SEQ
    x = jnp.concatenate([x_prompt, x_sample], axis=0).reshape(nb * SEQ, D_MODEL)
    for l in range(DEPTH):
        w_main, w_dil = _split_w_in(w_in[l])
        tables = _s5_tables(ssm_a_re[l], ssm_a_im[l], ssm_log_step[l], ssm_b_re[l], ssm_b_im[l], ssm_c_re[l],
                            ssm_c_im[l])
        x, h, aff_t = _mixer(
            x, nb, norm1_g[l], w_main, w_dil, pool_w[l], pool_scale[l], tables, ssm_d[l],
            ssm_glu_w[l].T.astype(BF16), w_br_attn[l].astype(BF16), w_br_pool[l].astype(BF16),
            w_br_ssm[l].astype(BF16), w_out[l].astype(BF16), norm2_g[l], router_w[l].T)
        wg, wu, wd = w_gate[l].astype(BF16), w_up[l].astype(BF16), w_down[l].astype(BF16)
        x = _experts(x, h, aff_t, tp, ts, wg, wu, wd)
    return (_final_norm(x, final_norm_g, 0, tp).reshape(nbp, SEQ, D_MODEL),
            _final_norm(x, final_norm_g, tp, ts).reshape(nbs, SEQ, D_MODEL))
```
